```python
import math
import jax, jax.numpy as jnp
from jax import lax
import numpy as np

D_MODEL = 1024
BATCH = 4
SEQ = 4096
DEPTH = 4
DEC_BATCH = 128
DEC_SEQ = 4
PAST_LEN = 2048
PAGE_SIZE = 128

HEAD_DIM = 64
GLA_HEADS = D_MODEL // (4 * HEAD_DIM)
GLA_DV = HEAD_DIM
GLA_DK = HEAD_DIM // 2
GLA_RANK = 16
GLA_TAU = 16.0
GDN_HEADS = D_MODEL // (4 * HEAD_DIM)
GDN_DK = HEAD_DIM
GDN_DV = HEAD_DIM
CONV_K = 4
SB_HEADS = D_MODEL // (2 * HEAD_DIM)
SB_DIM = HEAD_DIM
SB_BLOCK = 128
SB_BIAS_INIT = -6.0
CHUNK = 64
D_FF = 4 * D_MODEL
EPS = 1e-6

GLA_QK = GLA_HEADS * GLA_DK
GLA_V = GLA_HEADS * GLA_DV
GDN_QK = GDN_HEADS * GDN_DK
GDN_V = GDN_HEADS * GDN_DV
GDN_CONV_DIM = 2 * GDN_QK + GDN_V
SB_W = SB_HEADS * SB_DIM
MIX = GLA_V + GDN_V + SB_W
IN_SIZES = (GLA_QK, GLA_QK, GLA_V, GLA_RANK, GLA_V,
            GDN_CONV_DIM, GDN_HEADS, GDN_HEADS, GDN_V,
            SB_W, SB_W, SB_W)
N_IN = GLA_QK * 2 + GLA_V * 2 + GLA_RANK + GDN_CONV_DIM + GDN_HEADS * 2 + GDN_V + SB_W * 3

kernel_name = 'hymba_gla_gdn_stickbreak_decoder_step'


def rms_norm(x, g):
    xf = x.astype(jnp.float32)
    y = xf * lax.rsqrt(jnp.mean(xf * xf, axis=-1, keepdims=True) + EPS)
    return (y * g.astype(jnp.float32)).astype(x.dtype)


def l2_norm(x):
    xf = x.astype(jnp.float32)
    return xf * lax.rsqrt(jnp.sum(xf * xf, axis=-1, keepdims=True) + EPS)


def to_chunks(a, chunk):
    B, T, H, d = a.shape
    return a.reshape(B, T // chunk, chunk, H, d).transpose(1, 0, 3, 2, 4)


def from_chunks(o):
    n, B, H, C, d = o.shape
    return o.transpose(1, 0, 3, 2, 4).reshape(B, n * C, H, d)


def gla_chunked(q, k, v, log_a, s0, chunk):
    f32 = jnp.float32
    incl = jnp.tril(jnp.ones((chunk, chunk), bool))

    def step(s, inp):
        qc, kc, vc, gc = inp
        b = jnp.cumsum(gc, axis=2)
        o_inter = jnp.einsum('bhcd,bhde->bhce', qc * jnp.exp(b), s)
        diff = b[:, :, :, None, :] - b[:, :, None, :, :]
        decay = jnp.exp(jnp.where(incl[:, :, None], diff, -jnp.inf))
        att = jnp.einsum('bhtd,bhsd,bhtsd->bhts', qc, kc, decay)
        o = o_inter + jnp.einsum('bhts,bhse->bhte', att, vc)
        b_last = b[:, :, -1:, :]
        s = (jnp.exp(b_last[:, :, 0, :])[..., None] * s
             + jnp.einsum('bhcd,bhce->bhde', kc * jnp.exp(b_last - b), vc))
        return s, o

    xs = (to_chunks(q.astype(f32), chunk), to_chunks(k.astype(f32), chunk),
          to_chunks(v.astype(f32), chunk), to_chunks(log_a.astype(f32), chunk))
    s, o = lax.scan(step, s0.astype(f32), xs)
    return from_chunks(o), s


def gdn_chunked(q, k, v, g, beta, s0, chunk):
    f32 = jnp.float32
    dv = v.shape[-1]
    strict = jnp.tril(jnp.ones((chunk, chunk), bool), -1)
    incl = jnp.tril(jnp.ones((chunk, chunk), bool))
    eye = jnp.eye(chunk, dtype=f32)

    def step(s, inp):
        qc, kc, vc, gc, bc = inp
        b = jnp.cumsum(gc, axis=-1)
        diff = b[..., :, None] - b[..., None, :]
        g_strict = jnp.exp(jnp.where(strict, diff, -jnp.inf))
        g_incl = jnp.exp(jnp.where(incl, diff, -jnp.inf))
        kb = kc * bc[..., None]
        tri = eye + jnp.einsum('bhtd,bhsd->bhts', kb, kc) * g_strict
        rhs = jnp.concatenate([vc * bc[..., None], kb * jnp.exp(b)[..., None]], axis=-1)
        sol = lax.linalg.triangular_solve(tri, rhs, left_side=True, lower=True, unit_diagonal=True)
        u, w = sol[..., :dv], sol[..., dv:]
        v_new = u - jnp.einsum('bhcd,bhde->bhce', w, s)
        att = jnp.einsum('bhtd,bhsd->bhts', qc, kc) * g_incl
        o = (jnp.einsum('bhcd,bhde->bhce', qc * jnp.exp(b)[..., None], s)
             + jnp.einsum('bhts,bhse->bhte', att, v_new))
        b_last = b[..., -1:]
        s = (jnp.exp(b_last)[..., None] * s
             + jnp.einsum('bhcd,bhce->bhde', kc * jnp.exp(b_last - b)[..., None], v_new))
        return s, o

    xs = (to_chunks(q.astype(f32), chunk), to_chunks(k.astype(f32), chunk),
          to_chunks(v.astype(f32), chunk),
          to_chunks(g[..., None].astype(f32), chunk)[..., 0],
          to_chunks(beta[..., None].astype(f32), chunk)[..., 0])
    s, o = lax.scan(step, s0.astype(f32), xs)
    return from_chunks(o), s


def sb_core(q, k, v, q_pos, k_pos, bias):
    z = (jnp.einsum('bqhd,bkhd->bhqk', q.astype(jnp.float32), k.astype(jnp.float32)) * (SB_DIM ** -0.5)
         + bias.astype(jnp.float32)[None, :, None, None])
    mask = k_pos[None, :] < q_pos[:, None]
    log_beta = jax.nn.log_sigmoid(z)
    log_1m = jnp.where(mask, log_beta - z, 0.0)
    suffix = lax.cumsum(log_1m, axis=3, reverse=True) - log_1m
    att = jnp.where(mask, jnp.exp(log_beta + suffix), 0.0)
    return jnp.einsum('bhqk,bkhd->bqhd', att, v.astype(jnp.float32)).astype(v.dtype)


def sb_prompt(q, k, v, bias):
    B, T, H, D = q.shape
    nb = T // SB_BLOCK
    k_pos = jnp.arange(T)
    qb = q.reshape(B, nb, SB_BLOCK, H, D).transpose(1, 0, 2, 3, 4)

    def blk(args):
        q_blk, i = args
        return sb_core(q_blk, k, v, i * SB_BLOCK + jnp.arange(SB_BLOCK), k_pos, bias)

    o = lax.map(blk, (qb, jnp.arange(nb)))
    return o.transpose(1, 0, 2, 3, 4).reshape(B, T, H, D)


def sb_sample(q, k, v, past_k, past_v, bias):
    P = past_k.shape[1]
    T = q.shape[1]
    kk = jnp.concatenate([past_k.astype(k.dtype), k], axis=1)
    vv = jnp.concatenate([past_v.astype(v.dtype), v], axis=1)
    return sb_core(q, kk, vv, P + jnp.arange(T), jnp.arange(P + T), bias)


def layer(h, lp, gla_s0, gdn_s0, conv_buf, past_k, past_v):
    (ln1, w_in, gla_wa2, gla_ba, gla_ng, conv_w, a_log, dt_bias, gdn_ng,
     qn_g, kn_g, sb_bias, w_out, ln2, w_up, w_down) = lp
    B, T, _ = h.shape
    chunk = T if T <= CHUNK else CHUNK
    f32 = jnp.float32
    xn = rms_norm(h, ln1)
    proj = xn @ w_in
    (g_q, g_k, g_v, g_lr, g_z, d_qkv, d_b, d_a, d_z, s_q, s_k, s_v) = jnp.split(
        proj, np.cumsum(IN_SIZES)[:-1].tolist(), axis=-1)

    q = g_q.reshape(B, T, GLA_HEADS, GLA_DK) * (GLA_DK ** -0.5)
    k = g_k.reshape(B, T, GLA_HEADS, GLA_DK)
    v = g_v.reshape(B, T, GLA_HEADS, GLA_DV)
    log_a = jax.nn.log_sigmoid((g_lr @ gla_wa2 + gla_ba).astype(f32)) / GLA_TAU
    o, gla_s = gla_chunked(q, k, v, log_a.reshape(B, T, GLA_HEADS, GLA_DK), gla_s0, chunk)
    o_gla = rms_norm(o, gla_ng).astype(h.dtype) * jax.nn.silu(g_z.reshape(B, T, GLA_HEADS, GLA_DV))

    xp = jnp.concatenate([conv_buf.astype(d_qkv.dtype), d_qkv], axis=1)
    new_buf = xp[:, xp.shape[1] - (CONV_K - 1):]
    conv = xp[:, 0:T] * conv_w[0]
    for i in range(1, CONV_K):
        conv = conv + xp[:, i:i + T] * conv_w[i]
    conv = jax.nn.silu(conv)
    c_q, c_k, c_v = jnp.split(conv, [GDN_QK, 2 * GDN_QK], axis=-1)
    q = l2_norm(c_q.reshape(B, T, GDN_HEADS, GDN_DK)) * (GDN_DK ** -0.5)
    k = l2_norm(c_k.reshape(B, T, GDN_HEADS, GDN_DK))
    v = c_v.reshape(B, T, GDN_HEADS, GDN_DV)
    beta = jax.nn.sigmoid(d_b.astype(f32))
    g = -jnp.exp(a_log.astype(f32)) * jax.nn.softplus(d_a.astype(f32) + dt_bias.astype(f32))
    o, gdn_s = gdn_chunked(q, k, v, g, beta, gdn_s0, chunk)
    o_gdn = rms_norm(o, gdn_ng).astype(h.dtype) * jax.nn.silu(d_z.reshape(B, T, GDN_HEADS, GDN_DV))

    sq = rms_norm(s_q.reshape(B, T, SB_HEADS, SB_DIM), qn_g)
    sk = rms_norm(s_k.reshape(B, T, SB_HEADS, SB_DIM), kn_g)
    sv = s_v.reshape(B, T, SB_HEADS, SB_DIM)
    if past_k is None:
        o_sb = sb_prompt(sq, sk, sv, sb_bias)
    else:
        o_sb = sb_sample(sq, sk, sv, past_k, past_v, sb_bias)

    mixed = jnp.concatenate([o_gla.reshape(B, T, GLA_V), o_gdn.reshape(B, T, GDN_V),
                             o_sb.reshape(B, T, SB_W)], axis=-1)
    h = h + mixed @ w_out
    hn = rms_norm(h, ln2)
    h = h + jnp.square(jax.nn.relu(hn @ w_up)) @ w_down
    return h, (sk, sv, gla_s, gdn_s, new_buf)


def setup_inputs(seed: int = 0) -> dict:
    key = jax.random.key(seed)
    ks = jax.random.split(key, 32)
    f32 = jnp.float32

    def nrm(k, shape, s):
        return jax.random.normal(k, shape, f32) * s

    n_pages = PAST_LEN // PAGE_SIZE
    n_used = DEC_BATCH * n_pages
    n_pool = n_used + (n_used + 3) // 4
    page_table = jax.random.permutation(ks[0], n_pool)[:n_used].reshape(DEC_BATCH, n_pages).astype(jnp.int32)

    a = jax.random.uniform(ks[1], (DEPTH, GDN_HEADS), f32, 1.0, 16.0)
    dt = jnp.exp(jax.random.uniform(ks[2], (DEPTH, GDN_HEADS), f32, math.log(1e-3), math.log(1e-1)))
    return {
        'x_prompt': nrm(ks[3], (BATCH, SEQ, D_MODEL), 1.0),
        'x_sample': nrm(ks[4], (DEC_BATCH, DEC_SEQ, D_MODEL), 1.0),
        'cache_sb_k': nrm(ks[5], (DEPTH, n_pool, PAGE_SIZE, SB_HEADS, SB_DIM), 1.0),
        'cache_sb_v': nrm(ks[6], (DEPTH, n_pool, PAGE_SIZE, SB_HEADS, SB_DIM), 1.0),
        'page_table': page_table,
        'state_gla': nrm(ks[7], (DEPTH, DEC_BATCH, GLA_HEADS, GLA_DK, GLA_DV), 1.0),
        'state_gdn': nrm(ks[8], (DEPTH, DEC_BATCH, GDN_HEADS, GDN_DK, GDN_DV), 0.5),
        'state_gdn_conv': nrm(ks[9], (DEPTH, DEC_BATCH, CONV_K - 1, GDN_CONV_DIM), 1.0),
        'ln1_g': 1.0 + nrm(ks[10], (DEPTH, D_MODEL), 0.01),
        'w_in': nrm(ks[11], (DEPTH, D_MODEL, N_IN), D_MODEL ** -0.5),
        'gla_wa2': nrm(ks[12], (DEPTH, GLA_RANK, GLA_QK), GLA_RANK ** -0.5),
        'gla_ba': nrm(ks[13], (DEPTH, GLA_QK), 0.01),
        'gla_norm_g': 1.0 + nrm(ks[14], (DEPTH, GLA_DV), 0.01),
        'gdn_conv_w': nrm(ks[15], (DEPTH, CONV_K, GDN_CONV_DIM), CONV_K ** -0.5),
        'gdn_a_log': jnp.log(a),
        'gdn_dt_bias': dt + jnp.log(-jnp.expm1(-dt)),
        'gdn_norm_g': 1.0 + nrm(ks[16], (DEPTH, GDN_DV), 0.01),
        'sb_q_norm_g': 1.0 + nrm(ks[17], (DEPTH, SB_DIM), 0.01),
        'sb_k_norm_g': 1.0 + nrm(ks[18], (DEPTH, SB_DIM), 0.01),
        'sb_logit_bias': SB_BIAS_INIT + nrm(ks[23], (DEPTH, SB_HEADS), 0.1),
        'w_out': nrm(ks[19], (DEPTH, MIX, D_MODEL), MIX ** -0.5),
        'ln2_g': 1.0 + nrm(ks[20], (DEPTH, D_MODEL), 0.01),
        'w_up': nrm(ks[21], (DEPTH, D_MODEL, D_FF), D_MODEL ** -0.5),
        'w_down': nrm(ks[22], (DEPTH, D_FF, D_MODEL), D_FF ** -0.5),
    }


def reference(x_prompt, x_sample, cache_sb_k, cache_sb_v, page_table, state_gla, state_gdn, state_gdn_conv,
              ln1_g, w_in, gla_wa2, gla_ba, gla_norm_g, gdn_conv_w, gdn_a_log, gdn_dt_bias, gdn_norm_g,
              sb_q_norm_g, sb_k_norm_g, sb_logit_bias, w_out, ln2_g, w_up, w_down):
    n_db, n_pages = page_table.shape
    past_len = n_pages * PAGE_SIZE
    bp = x_prompt.shape[0]
    hp, hs = x_prompt, x_sample
    st_prompt, st_sample = [], []
    for l in range(DEPTH):
        lp = (ln1_g[l], w_in[l], gla_wa2[l], gla_ba[l], gla_norm_g[l], gdn_conv_w[l], gdn_a_log[l],
              gdn_dt_bias[l], gdn_norm_g[l], sb_q_norm_g[l], sb_k_norm_g[l], sb_logit_bias[l], w_out[l],
              ln2_g[l], w_up[l], w_down[l])
        hp, sp = layer(hp, lp,
                       jnp.zeros((bp, GLA_HEADS, GLA_DK, GLA_DV), jnp.float32),
                       jnp.zeros((bp, GDN_HEADS, GDN_DK, GDN_DV), jnp.float32),
                       jnp.zeros((bp, CONV_K - 1, GDN_CONV_DIM), x_prompt.dtype),
                       None, None)
        pk = cache_sb_k[l][page_table].reshape(n_db, past_len, SB_HEADS, SB_DIM)
        pv = cache_sb_v[l][page_table].reshape(n_db, past_len, SB_HEADS, SB_DIM)
        hs, ss = layer(hs, lp, state_gla[l], state_gdn[l], state_gdn_conv[l], pk, pv)
        st_prompt.append(sp)
        st_sample.append(ss)
    sbk_p, sbv_p, gla_p, gdn_p, conv_p = [jnp.stack(a) for a in zip(*st_prompt)]
    sbk_s, sbv_s, gla_s, gdn_s, conv_s = [jnp.stack(a) for a in zip(*st_sample)]
    return (hp, hs, sbk_p, sbv_p, gla_p, gdn_p, conv_p, sbk_s, sbv_s, gla_s, gdn_s, conv_s)
```

```python
import functools
import math

import jax
import jax.numpy as jnp
from jax import lax
from jax.experimental import pallas as pl
from jax.experimental.pallas import tpu as pltpu

F32 = jnp.float32
BF16 = jnp.bfloat16
HI = lax.Precision.HIGHEST

EPS = 1e-6
HEAD_DIM = 64
GLA_HEADS = 4
GLA_DK = 32
GLA_DV = 64
GLA_RANK = 16
GLA_TAU = 16.0
GDN_HEADS = 4
GDN_DK = 64
GDN_DV = 64
CONV_K = 4
SB_HEADS = 8
SB_DIM = 64
PAGE_SIZE = 128
CHUNK = 64

GLA_QK = GLA_HEADS * GLA_DK
GLA_V = GLA_HEADS * GLA_DV
GDN_QK = GDN_HEADS * GDN_DK
GDN_V = GDN_HEADS * GDN_DV
GDN_CONV_DIM = 2 * GDN_QK + GDN_V
SB_W = SB_HEADS * SB_DIM

R_DQKV = 0
R_GV = 768
R_GZ = 1024
R_DZ = 1280
R_GQ = 1536
R_GK = 1664
R_SMALL = 1792
R_WIDTH = 1920
LANE_LR = 0
LANE_DB = 16
LANE_DA = 20
N_PAD = 3 * SB_W + R_WIDTH

VMEM_LIMIT = 56 * 1024 * 1024


def _dot(a, b, precision=None):
    return jnp.dot(a, b, precision=precision, preferred_element_type=F32)


def _dot_nt(a, b, precision=None):
    return lax.dot_general(a, b, (((1,), (1,)), ((), ())), precision=precision, preferred_element_type=F32)


def _dot_tn(a, b, precision=None):
    return lax.dot_general(a, b, (((0,), (0,)), ((), ())), precision=precision, preferred_element_type=F32)


def _softplus(x):
    return jnp.maximum(x, 0.0) + jnp.log1p(jnp.exp(-jnp.abs(x)))


def _sigmoid(x):
    return 1.0 / (1.0 + jnp.exp(-x))


def _silu(x):
    return x * _sigmoid(x)


def _headnorm(x, g):
    outs = []
    lane = lax.broadcasted_iota(jnp.int32, (1, 128), 1)
    first = lane < HEAD_DIM
    for j in range(x.shape[1] // 128):
        xb = x[:, j * 128:(j + 1) * 128]
        sq = xb * xb
        s0 = jnp.sum(jnp.where(first, sq, 0.0), axis=-1, keepdims=True)
        s1 = jnp.sum(jnp.where(first, 0.0, sq), axis=-1, keepdims=True)
        ms = jnp.where(first, s0, s1) * (1.0 / HEAD_DIM)
        outs.append(xb * lax.rsqrt(ms + EPS))
    return jnp.concatenate(outs, axis=-1) * g


def _in_proj_kernel(x_ref, g_ref, w_ref, qg_ref, kg_ref, q_ref, k_ref, v_ref, r_ref):
    x = x_ref[...]
    ms = jnp.mean(x * x, axis=-1, keepdims=True)
    xn = (x * lax.rsqrt(ms + EPS) * g_ref[...]).astype(BF16)
    sq = _dot(xn, w_ref[:, 0:SB_W])
    q_ref[...] = (_headnorm(sq, qg_ref[...]) * (SB_DIM ** -0.5)).astype(BF16)
    sk = _dot(xn, w_ref[:, SB_W:2 * SB_W])
    k_ref[...] = _headnorm(sk, kg_ref[...])
    v_ref[...] = _dot(xn, w_ref[:, 2 * SB_W:3 * SB_W])
    for c0 in range(0, R_WIDTH, 640):
        r_ref[:, c0:c0 + 640] = _dot(xn, w_ref[:, 3 * SB_W + c0:3 * SB_W + c0 + 640])


def _in_proj(h, ln_g, w_pad, qn_g, kn_g, tm):
    m = h.shape[0]
    row = lambda i: (i, 0)
    const = lambda i: (0, 0)
    return pl.pallas_call(
        _in_proj_kernel,
        grid=(m // tm,),
        in_specs=[
            pl.BlockSpec((tm, h.shape[1]), row),
            pl.BlockSpec((1, h.shape[1]), const),
            pl.BlockSpec(w_pad.shape, const, pipeline_mode=pl.Buffered(1)),
            pl.BlockSpec((1, SB_W), const),
            pl.BlockSpec((1, SB_W), const),
        ],
        out_specs=[
            pl.BlockSpec((tm, SB_W), row),
            pl.BlockSpec((tm, SB_W), row),
            pl.BlockSpec((tm, SB_W), row),
            pl.BlockSpec((tm, R_WIDTH), row),
        ],
        out_shape=[
            jax.ShapeDtypeStruct((m, SB_W), BF16),
            jax.ShapeDtypeStruct((m, SB_W), F32),
            jax.ShapeDtypeStruct((m, SB_W), F32),
            jax.ShapeDtypeStruct((m, R_WIDTH), F32),
        ],
        compiler_params=pltpu.CompilerParams(dimension_semantics=("arbitrary",), vmem_limit_bytes=VMEM_LIMIT),
        name="in_proj",
    )(h, ln_g, w_pad, qn_g, kn_g)


def _out_ffn_kernel(h_ref, og_ref, od_ref, os_ref, wo_ref, g_ref, wu_ref, wd_ref, o_ref, *, f_chunk):
    mixed = jnp.concatenate([og_ref[...], od_ref[...], os_ref[...]], axis=-1)
    h = h_ref[...] + _dot(mixed, wo_ref[...])
    ms = jnp.mean(h * h, axis=-1, keepdims=True)
    hn = (h * lax.rsqrt(ms + EPS) * g_ref[...]).astype(BF16)
    d_ff = wu_ref.shape[1]
    acc = h
    for f0 in range(0, d_ff, f_chunk):
        u = _dot(hn, wu_ref[:, f0:f0 + f_chunk])
        u = jnp.maximum(u, 0.0)
        acc = acc + _dot((u * u).astype(BF16), wd_ref[f0:f0 + f_chunk, :])
    o_ref[...] = acc


def _out_ffn(h, og, od, osb, wo, ln_g, wu, wd, tm):
    m, d = h.shape
    row = lambda i: (i, 0)
    const = lambda i: (0, 0)
    return pl.pallas_call(
        functools.partial(_out_ffn_kernel, f_chunk=1024),
        grid=(m // tm,),
        in_specs=[
            pl.BlockSpec((tm, d), row),
            pl.BlockSpec((tm, GLA_V), row),
            pl.BlockSpec((tm, GDN_V), row),
            pl.BlockSpec((tm, SB_W), row),
            pl.BlockSpec(wo.shape, const, pipeline_mode=pl.Buffered(1)),
            pl.BlockSpec((1, d), const),
            pl.BlockSpec(wu.shape, const, pipeline_mode=pl.Buffered(1)),
            pl.BlockSpec(wd.shape, const, pipeline_mode=pl.Buffered(1)),
        ],
        out_specs=pl.BlockSpec((tm, d), row),
        out_shape=jax.ShapeDtypeStruct((m, d), F32),
        compiler_params=pltpu.CompilerParams(dimension_semantics=("arbitrary",), vmem_limit_bytes=VMEM_LIMIT),
        name="out_ffn",
    )(h, og, od, osb, wo, ln_g, wu, wd)


def _row_col(c):
    row = lax.broadcasted_iota(jnp.int32, (c, c), 0)
    col = lax.broadcasted_iota(jnp.int32, (c, c), 1)
    return row, col


def _gla_kernel(q_ref, k_ref, v_ref, z_ref, sm_ref, s0_ref, wa2_ref, ba_ref, ng_ref, gmean_ref,
                o_ref, sout_ref, st_scr, *, chunk, t_valid):
    c = chunk

    @pl.when(pl.program_id(1) == 0)
    def _():
        st_scr[...] = s0_ref[...]

    q = q_ref[...] * (GLA_DK ** -0.5)
    k = k_ref[...]
    v = v_ref[...]
    x = _dot(sm_ref[...], wa2_ref[...], HI) + ba_ref[...]
    log_a = -_softplus(-x) * (1.0 / GLA_TAU)
    if t_valid < c:
        valid = lax.broadcasted_iota(jnp.int32, (c, 1), 0) < t_valid
        log_a = jnp.where(valid, log_a, 0.0)
        q = jnp.where(valid, q, 0.0)
        k = jnp.where(valid, k, 0.0)
        v = jnp.where(valid, v, 0.0)
    row, col = _row_col(c)
    incl = row >= col
    b = _dot(incl.astype(F32), log_a, HI)
    b_last = b[c - 1:c, :]
    b_mid = b[c // 2 - 1:c // 2, :]
    q_in = (q * jnp.exp(b)).astype(BF16)
    q_t = q * jnp.exp(b - b_mid)
    k_t = (k * jnp.exp(b_mid - b)).astype(BF16)
    k_st = (k * jnp.exp(b_last - b)).astype(BF16)
    vb = v.astype(BF16)

    st = st_scr[...]
    o = _dot_nt(q_in, st.astype(BF16))
    lane_k = lax.broadcasted_iota(jnp.int32, (1, GLA_QK), 1) // GLA_DK
    lane_v = lax.broadcasted_iota(jnp.int32, (1, GLA_V), 1) // GLA_DV
    for h in range(GLA_HEADS):
        qh = jnp.where(lane_k == h, q_t, 0.0).astype(BF16)
        att = jnp.where(incl, _dot_nt(qh, k_t), 0.0).astype(BF16)
        o = o + _dot(att, jnp.where(lane_v == h, vb, jnp.zeros_like(vb)))

    bd = (lax.broadcasted_iota(jnp.int32, (GLA_V, GLA_QK), 0) // GLA_DV
          == lax.broadcasted_iota(jnp.int32, (GLA_V, GLA_QK), 1) // GLA_DK)
    st_new = st * jnp.exp(b_last) + jnp.where(bd, _dot_tn(vb, k_st), 0.0)
    st_scr[...] = st_new
    sout_ref[...] = st_new

    ms = _dot(o * o, gmean_ref[...], HI)
    y = o * lax.rsqrt(ms + EPS) * ng_ref[...]
    o_ref[...] = (y * _silu(z_ref[...])).astype(BF16)


def _gla(rest, s0t, wa2_pad, ba, ng_t, gmean, n_seq, seq_len, chunk, t_valid):
    nc = seq_len // chunk
    blk = lambda width, cb: pl.BlockSpec((chunk, width), lambda b, c: (b * nc + c, cb))
    const2 = lambda b, c: (0, 0)
    return pl.pallas_call(
        functools.partial(_gla_kernel, chunk=chunk, t_valid=t_valid),
        grid=(n_seq, nc),
        in_specs=[
            blk(GLA_QK, R_GQ // GLA_QK),
            blk(GLA_QK, R_GK // GLA_QK),
            blk(GLA_V, R_GV // GLA_V),
            blk(GLA_V, R_GZ // GLA_V),
            blk(128, R_SMALL // 128),
            pl.BlockSpec((None, GLA_V, GLA_QK), lambda b, c: (b, 0, 0)),
            pl.BlockSpec((128, GLA_QK), const2),
            pl.BlockSpec((1, GLA_QK), const2),
            pl.BlockSpec((1, GLA_V), const2),
            pl.BlockSpec((GLA_V, GLA_V), const2),
        ],
        out_specs=[
            pl.BlockSpec((chunk, GLA_V), lambda b, c: (b * nc + c, 0)),
            pl.BlockSpec((None, GLA_V, GLA_QK), lambda b, c: (b, 0, 0)),
        ],
        out_shape=[
            jax.ShapeDtypeStruct((n_seq * seq_len, GLA_V), BF16),
            jax.ShapeDtypeStruct((n_seq, GLA_V, GLA_QK), F32),
        ],
        scratch_shapes=[pltpu.VMEM((GLA_V, GLA_QK), F32)],
        compiler_params=pltpu.CompilerParams(dimension_semantics=("arbitrary", "arbitrary")),
        name="gla",
    )(rest, rest, rest, rest, rest, s0t, wa2_pad, ba, ng_t, gmean)


def _gdn_kernel(x_ref, sm_ref, z_ref, cb0_ref, s0_ref, cw_ref, alog_ref, dtb_ref, ng_ref, gsum_ref,
                expb_ref, expg_ref, o_ref, sout_ref, xp_scr, s_scr, *, chunk, t_valid):
    c = chunk

    @pl.when(pl.program_id(1) == 0)
    def _():
        xp_scr[0:8, :] = cb0_ref[...]
        s_scr[...] = s0_ref[...]

    xp_scr[8:8 + c, :] = x_ref[...]
    base = 8 - (CONV_K - 1)
    conv = xp_scr[base:base + c, :] * cw_ref[0:1, :]
    for i in range(1, CONV_K):
        conv = conv + xp_scr[base + i:base + i + c, :] * cw_ref[i:i + 1, :]
    xp_scr[0:8, :] = xp_scr[c:c + 8, :]
    conv = _silu(conv)
    cq = conv[:, 0:GDN_QK]
    ck = conv[:, GDN_QK:2 * GDN_QK]
    v = conv[:, 2 * GDN_QK:]
    gsum = gsum_ref[...]
    q = cq * lax.rsqrt(_dot(cq * cq, gsum, HI) + EPS) * (GDN_DK ** -0.5)
    k = ck * lax.rsqrt(_dot(ck * ck, gsum, HI) + EPS)

    sm = sm_ref[...]
    beta_all = _sigmoid(sm)
    g_all = -jnp.exp(alog_ref[...]) * _softplus(sm + dtb_ref[...])
    if t_valid < c:
        valid = lax.broadcasted_iota(jnp.int32, (c, 1), 0) < t_valid
        q = jnp.where(valid, q, 0.0)
        k = jnp.where(valid, k, 0.0)
        v = jnp.where(valid, v, 0.0)
        beta_all = jnp.where(valid, beta_all, 0.0)
        g_all = jnp.where(valid, g_all, 0.0)
    row, col = _row_col(c)
    incl = row >= col
    strict = row > col
    b_col = _dot(incl.astype(F32), g_all, HI)
    bfull = _dot(b_col, expg_ref[...], HI)
    beta_full = _dot(beta_all, expb_ref[...], HI)
    b_last = bfull[c - 1:c, :]
    eb = jnp.exp(bfull)
    kb = k * beta_full
    rv = v * beta_full
    rk = kb * eb
    q_in = (q * eb).astype(BF16)
    k_st = (k * jnp.exp(b_last - bfull)).astype(BF16)
    kbf = k.astype(BF16)

    lane_h = lax.broadcasted_iota(jnp.int32, (1, GDN_V), 1) // GDN_DV
    lane_s = lax.broadcasted_iota(jnp.int32, (c, 128), 1)
    eye = (row == col).astype(F32)
    n_sq = int(round(math.log2(c))) - 1
    u = jnp.zeros((c, GDN_V), F32)
    w = jnp.zeros((c, GDN_V), F32)
    atts = []
    for h in range(GDN_HEADS):
        mh = lane_h == h
        sel = (lane_s == LANE_DA + h).astype(F32)
        b_s = _dot_nt(sel, b_col, HI)
        b_t = jnp.broadcast_to(b_col[:, LANE_DA + h:LANE_DA + h + 1], (c, c))
        dec = jnp.exp(jnp.minimum(b_t - b_s, 0.0))
        a = jnp.where(strict, _dot_nt(jnp.where(mh, kb, 0.0), k, HI) * dec, 0.0)
        p = eye - a
        qq = a
        for _ in range(n_sq):
            qq = _dot(qq, qq, HI)
            p = p + _dot(p, qq, HI)
        u = u + _dot(p, jnp.where(mh, rv, 0.0), HI)
        w = w + _dot(p, jnp.where(mh, rk, 0.0), HI)
        att = _dot_nt(jnp.where(mh, q, 0.0).astype(BF16), kbf)
        atts.append(jnp.where(incl, att * dec, 0.0).astype(BF16))

    s = s_scr[...]
    sb = s.astype(BF16)
    v_new = u - _dot(w.astype(BF16), sb)
    vnb = v_new.astype(BF16)
    o = _dot(q_in, sb)
    for h in range(GDN_HEADS):
        o = o + _dot(atts[h], jnp.where(lane_h == h, vnb, jnp.zeros_like(vnb)))
    bd = (lax.broadcasted_iota(jnp.int32, (GDN_QK, GDN_V), 0) // GDN_DK
          == lax.broadcasted_iota(jnp.int32, (GDN_QK, GDN_V), 1) // GDN_DV)
    s_new = s * jnp.exp(b_last) + jnp.where(bd, _dot_tn(k_st, vnb), 0.0)
    s_scr[...] = s_new
    sout_ref[...] = s_new

    gmean = gsum * (1.0 / GDN_DV)
    ms = _dot(o * o, gmean, HI)
    y = o * lax.rsqrt(ms + EPS) * ng_ref[...]
    o_ref[...] = (y * _silu(z_ref[...])).astype(BF16)


def _gdn(rest, cb0, s0, cw, alog_row, dtb_row, ng_t, gsum, expb, expg, n_seq, seq_len, chunk, t_valid):
    nc = seq_len // chunk
    blk = lambda width, cb: pl.BlockSpec((chunk, width), lambda b, c: (b * nc + c, cb))
    const2 = lambda b, c: (0, 0)
    return pl.pallas_call(
        functools.partial(_gdn_kernel, chunk=chunk, t_valid=t_valid),
        grid=(n_seq, nc),
        in_specs=[
            blk(GDN_CONV_DIM, R_DQKV // GDN_CONV_DIM),
            blk(128, R_SMALL // 128),
            blk(GDN_V, R_DZ // GDN_V),
            pl.BlockSpec((None, 8, GDN_CONV_DIM), lambda b, c: (b, 0, 0)),
            pl.BlockSpec((None, GDN_QK, GDN_V), lambda b, c: (b, 0, 0)),
            pl.BlockSpec((CONV_K, GDN_CONV_DIM), const2),
            pl.BlockSpec((1, 128), const2),
            pl.BlockSpec((1, 128), const2),
            pl.BlockSpec((1, GDN_V), const2),
            pl.BlockSpec((GDN_V, GDN_V), const2),
            pl.BlockSpec((128, GDN_V), const2),
            pl.BlockSpec((128, GDN_V), const2),
        ],
        out_specs=[
            pl.BlockSpec((chunk, GDN_V), lambda b, c: (b * nc + c, 0)),
            pl.BlockSpec((None, GDN_QK, GDN_V), lambda b, c: (b, 0, 0)),
        ],
        out_shape=[
            jax.ShapeDtypeStruct((n_seq * seq_len, GDN_V), BF16),
            jax.ShapeDtypeStruct((n_seq, GDN_QK, GDN_V), F32),
        ],
        scratch_shapes=[pltpu.VMEM((chunk + 8, GDN_CONV_DIM), F32), pltpu.VMEM((GDN_QK, GDN_V), F32)],
        compiler_params=pltpu.CompilerParams(dimension_semantics=("arbitrary", "arbitrary")),
        name="gdn",
    )(rest, rest, rest, cb0, s0, cw, alog_row, dtb_row, ng_t, gsum, expb, expg)


def _sb_tile(z, c, mask, mu):
    tk = z.shape[1]
    l = jnp.log1p(jnp.exp(-jnp.abs(z)))
    log_beta = jnp.minimum(z, 0.0) - l
    log_1m = log_beta - z
    if mask is not None:
        log_1m = jnp.where(mask, log_1m, 0.0)
    sc = _dot(log_1m.astype(BF16), mu)
    att = jnp.exp(log_beta + sc[:, :tk] + c)
    if mask is not None:
        att = jnp.where(mask, att, 0.0)
    return att.astype(BF16), c + sc[:, tk:]


def _sb_prompt_kernel(bias_ref, q_ref, k_ref, v_ref, mu_ref, o_ref, kb_scr, vb_scr, *, tq):
    hp = pl.program_id(1)
    i = pl.program_id(2)

    @pl.when(i == 0)
    def _():
        kb_scr[...] = k_ref[...].astype(BF16)
        vb_scr[...] = v_ref[...].astype(BF16)

    q = q_ref[...]
    lane = lax.broadcasted_iota(jnp.int32, (1, 2 * SB_DIM), 1)
    first = lane < SB_DIM
    zero = jnp.zeros_like(q)
    qs = (jnp.where(first, q, zero), jnp.where(first, zero, q))
    bias = (bias_ref[2 * hp], bias_ref[2 * hp + 1])
    mu = mu_ref[...]
    row, col = _row_col(tq)
    strict = row > col

    def tile(j, carry, mask):
        start = pl.multiple_of(j * tq, tq)
        kblk = kb_scr[pl.ds(start, tq), :]
        vblk = vb_scr[pl.ds(start, tq), :]
        out = []
        for hh in range(2):
            c, o = carry[hh]
            z = _dot_nt(qs[hh], kblk) + bias[hh]
            att, c = _sb_tile(z, c, mask, mu)
            out.append((c, o + _dot(att, vblk)))
        return tuple(out)

    zc = jnp.zeros((tq, tq), F32)
    zo = jnp.zeros((tq, 2 * SB_DIM), F32)
    carry = tile(i, ((zc, zo), (zc, zo)), strict)
    carry = lax.fori_loop(0, i, lambda jj, cr: tile(i - 1 - jj, cr, None), carry)
    o_ref[...] = jnp.where(first, carry[0][1], carry[1][1]).astype(BF16)


def _sb_prompt(qn, sk, sv, bias, mu, n_seq, seq_len, tq):
    nq = seq_len // tq
    n_hp = SB_HEADS // 2
    grid_spec = pltpu.PrefetchScalarGridSpec(
        num_scalar_prefetch=1,
        grid=(n_seq, n_hp, nq),
        in_specs=[
            pl.BlockSpec((tq, 2 * SB_DIM), lambda b, hp, i, bias: (b * nq + i, hp)),
            pl.BlockSpec((seq_len, 2 * SB_DIM), lambda b, hp, i, bias: (b, hp)),
            pl.BlockSpec((seq_len, 2 * SB_DIM), lambda b, hp, i, bias: (b, hp)),
            pl.BlockSpec((tq, 2 * tq), lambda b, hp, i, bias: (0, 0)),
        ],
        out_specs=pl.BlockSpec((tq, 2 * SB_DIM), lambda b, hp, i, bias: (b * nq + i, hp)),
        scratch_shapes=[pltpu.VMEM((seq_len, 2 * SB_DIM), BF16), pltpu.VMEM((seq_len, 2 * SB_DIM), BF16)],
    )
    return pl.pallas_call(
        functools.partial(_sb_prompt_kernel, tq=tq),
        grid_spec=grid_spec,
        out_shape=jax.ShapeDtypeStruct((n_seq * seq_len, SB_W), BF16),
        compiler_params=pltpu.CompilerParams(dimension_semantics=("arbitrary", "arbitrary", "arbitrary")),
        name="sb_prompt",
    )(bias, qn, sk, sv, mu)


def _sb_sample_kernel(pt_ref, qbd_ref, brow_ref, kn_ref, vn_ref, mu_ref, *rest, n_pages, t_new):
    k_refs = rest[:n_pages]
    v_refs = rest[n_pages:2 * n_pages]
    o_ref = rest[2 * n_pages]
    del pt_ref
    qbd = qbd_ref[...]
    rows = qbd.shape[0]
    bias = brow_ref[...]
    mu = mu_ref[...]
    pad = jnp.zeros((PAGE_SIZE - 8, SB_W), BF16)
    kn = jnp.concatenate([kn_ref[...].astype(BF16), pad], axis=0)
    vn = jnp.concatenate([vn_ref[...].astype(BF16), pad], axis=0)
    t_of_row = lax.broadcasted_iota(jnp.int32, (rows, PAGE_SIZE), 0) // SB_HEADS
    key = lax.broadcasted_iota(jnp.int32, (rows, PAGE_SIZE), 1)
    mask = key < t_of_row
    c = jnp.zeros((rows, PAGE_SIZE), F32)
    att, c = _sb_tile(_dot_nt(qbd, kn) + bias, c, mask, mu)
    acc = _dot(att, vn)
    for p in range(n_pages - 1, -1, -1):
        att, c = _sb_tile(_dot_nt(qbd, k_refs[p][...].astype(BF16)) + bias, c, None, mu)
        acc = acc + _dot(att, v_refs[p][...].astype(BF16))
    h_of_row = lax.broadcasted_iota(jnp.int32, (rows, SB_W), 0) % SB_HEADS
    h_of_lane = lax.broadcasted_iota(jnp.int32, (rows, SB_W), 1) // SB_DIM
    acc = jnp.where(h_of_row == h_of_lane, acc, 0.0)
    o_ref[...] = jnp.sum(acc.reshape(t_new, SB_HEADS, SB_W), axis=1)


def _sb_sample(page_table, qbd, bias_rows, kn, vn, mu, cache_k, cache_v, layer):
    n_b, n_pages = page_table.shape
    rows = qbd.shape[1]
    t_new = rows // SB_HEADS

    def page_spec(p):
        return pl.BlockSpec((None, None, PAGE_SIZE, SB_W), lambda b, pt: (layer, pt[b, p], 0, 0))

    grid_spec = pltpu.PrefetchScalarGridSpec(
        num_scalar_prefetch=1,
        grid=(n_b,),
        in_specs=[
            pl.BlockSpec((None, rows, SB_W), lambda b, pt: (b, 0, 0)),
            pl.BlockSpec((rows, PAGE_SIZE), lambda b, pt: (0, 0)),
            pl.BlockSpec((None, 8, SB_W), lambda b, pt: (b, 0, 0)),
            pl.BlockSpec((None, 8, SB_W), lambda b, pt: (b, 0, 0)),
            pl.BlockSpec((PAGE_SIZE, 2 * PAGE_SIZE), lambda b, pt: (0, 0)),
        ] + [page_spec(p) for p in range(n_pages)] + [page_spec(p) for p in range(n_pages)],
        out_specs=pl.BlockSpec((None, t_new, SB_W), lambda b, pt: (b, 0, 0)),
    )
    return pl.pallas_call(
        functools.partial(_sb_sample_kernel, n_pages=n_pages, t_new=t_new),
        grid_spec=grid_spec,
        out_shape=jax.ShapeDtypeStruct((n_b, t_new, SB_W), F32),
        compiler_params=pltpu.CompilerParams(dimension_semantics=("arbitrary",), vmem_limit_bytes=VMEM_LIMIT),
        name="sb_sample",
    )(page_table, qbd, bias_rows, kn, vn, mu, *([cache_k] * n_pages), *([cache_v] * n_pages))


def _regroup_w_in(w_in):
    sizes = (GLA_QK, GLA_QK, GLA_V, GLA_RANK, GLA_V, GDN_CONV_DIM, GDN_HEADS, GDN_HEADS, GDN_V, SB_W, SB_W, SB_W)
    offs = [0]
    for s in sizes:
        offs.append(offs[-1] + s)
    (g_q, g_k, g_v, g_lr, g_z, d_qkv, d_b, d_a, d_z, s_q, s_k, s_v) = [
        w_in[:, offs[i]:offs[i + 1]] for i in range(len(sizes))]
    d = w_in.shape[0]
    small = jnp.concatenate([g_lr, d_b, d_a, jnp.zeros((d, 128 - GLA_RANK - 2 * GDN_HEADS), w_in.dtype)], axis=1)
    return jnp.concatenate([s_q, s_k, s_v, d_qkv, g_v, g_z, d_z, g_q, g_k, small], axis=1).astype(BF16)


def _block_diag_expand(s, transpose):
    bsz, h, a, b = s.shape
    if transpose:
        s = jnp.swapaxes(s, 2, 3)
        a, b = b, a
    eye = jnp.eye(h, dtype=s.dtype)
    return (s[:, :, :, None, :] * eye[None, :, None, :, None]).reshape(bsz, h * a, h * b)


def _block_diag_extract(s, h, transpose):
    bsz, ra, rb = s.shape
    a, b = ra // h, rb // h
    s = s.reshape(bsz, h, a, h, b)
    s = jnp.stack([s[:, i, :, i, :] for i in range(h)], axis=1)
    return jnp.swapaxes(s, 2, 3) if transpose else s


def _group_matrix(n, group, scale):
    idx = jnp.arange(n) // group
    return (idx[:, None] == idx[None, :]).astype(F32) * scale


def _lane_expand(first_lane, heads, width):
    src = jnp.arange(128)[:, None]
    dst_h = jnp.arange(heads * width)[None, :] // width
    return (src == first_lane + dst_h).astype(F32)


def _suffix_matrix(tk):
    j = jnp.arange(tk)[:, None]
    s = jnp.arange(tk)[None, :]
    return jnp.concatenate([(j > s), jnp.ones((tk, tk), bool)], axis=1).astype(BF16)


def _pad_rows(a, n_seq, t, t_pad):
    w = a.shape[-1]
    a = a.reshape(n_seq, t, w)
    return jnp.pad(a, ((0, 0), (0, t_pad - t), (0, 0))).reshape(n_seq * t_pad, w)


def kernel(x_prompt, x_sample, cache_sb_k, cache_sb_v, page_table, state_gla, state_gdn, state_gdn_conv,
           ln1_g, w_in, gla_wa2, gla_ba, gla_norm_g, gdn_conv_w, gdn_a_log, gdn_dt_bias, gdn_norm_g,
           sb_q_norm_g, sb_k_norm_g, sb_logit_bias, w_out, ln2_g, w_up, w_down):
    depth = w_in.shape[0]
    bp, tp, d = x_prompt.shape
    bs, ts, _ = x_sample.shape
    n_pool = cache_sb_k.shape[1]
    ts_pad = 8
    assert ts <= ts_pad and ts >= CONV_K - 1 and tp % CHUNK == 0

    hp = x_prompt.reshape(bp * tp, d)
    hs = x_sample.reshape(bs * ts, d)
    cache_k = cache_sb_k.reshape(depth, n_pool, PAGE_SIZE, SB_W)
    cache_v = cache_sb_v.reshape(depth, n_pool, PAGE_SIZE, SB_W)

    gmean = _group_matrix(GLA_V, GLA_DV, 1.0 / GLA_DV)
    gsum = _group_matrix(GDN_QK, GDN_DK, 1.0)
    expb = _lane_expand(LANE_DB, GDN_HEADS, GDN_DV)
    expg = _lane_expand(LANE_DA, GDN_HEADS, GDN_DV)
    mu = _suffix_matrix(PAGE_SIZE)
    zeros_gla_p = jnp.zeros((bp, GLA_V, GLA_QK), F32)
    zeros_gdn_p = jnp.zeros((bp, GDN_QK, GDN_V), F32)
    zeros_cb_p = jnp.zeros((bp, 8, GDN_CONV_DIM), F32)

    outs_p, outs_s = [], []
    for l in range(depth):
        w_pad = _regroup_w_in(w_in[l])
        wo = w_out[l].astype(BF16)
        wu = w_up[l].astype(BF16)
        wd = w_down[l].astype(BF16)
        ln1 = ln1_g[l][None, :]
        ln2 = ln2_g[l][None, :]
        qg = jnp.tile(sb_q_norm_g[l], SB_HEADS)[None, :]
        kg = jnp.tile(sb_k_norm_g[l], SB_HEADS)[None, :]
        wa2_pad = jnp.zeros((128, GLA_QK), F32).at[LANE_LR:LANE_LR + GLA_RANK].set(gla_wa2[l])
        ba = gla_ba[l][None, :]
        gla_ng = jnp.tile(gla_norm_g[l], GLA_HEADS)[None, :]
        gdn_ng = jnp.tile(gdn_norm_g[l], GDN_HEADS)[None, :]
        alog_row = jnp.zeros((1, 128), F32).at[0, LANE_DA:LANE_DA + GDN_HEADS].set(gdn_a_log[l])
        dtb_row = jnp.zeros((1, 128), F32).at[0, LANE_DA:LANE_DA + GDN_HEADS].set(gdn_dt_bias[l])
        cw = gdn_conv_w[l]
        bias = sb_logit_bias[l]

        qn, sk, sv, rest = _in_proj(hp, ln1, w_pad, qg, kg, 512)
        og, gla_st = _gla(rest, zeros_gla_p, wa2_pad, ba, gla_ng, gmean, bp, tp, CHUNK, CHUNK)
        od, gdn_st = _gdn(rest, zeros_cb_p, zeros_gdn_p, cw, alog_row, dtb_row, gdn_ng, gsum, expb, expg,
                          bp, tp, CHUNK, CHUNK)
        osb = _sb_prompt(qn, sk, sv, bias, mu, bp, tp, PAGE_SIZE)
        hp = _out_ffn(hp, og, od, osb, wo, ln2, wu, wd, 512)
        conv_p = rest.reshape(bp, tp, R_WIDTH)[:, tp - (CONV_K - 1):, R_DQKV:R_DQKV + GDN_CONV_DIM]
        outs_p.append((sk.reshape(bp, tp, SB_HEADS, SB_DIM), sv.reshape(bp, tp, SB_HEADS, SB_DIM),
                       _block_diag_extract(gla_st, GLA_HEADS, True),
                       _block_diag_extract(gdn_st, GDN_HEADS, False), conv_p))

        qn, sk, sv, rest = _in_proj(hs, ln1, w_pad, qg, kg, bs * ts)
        rest_pad = _pad_rows(rest, bs, ts, ts_pad)
        og, gla_st = _gla(rest_pad, _block_diag_expand(state_gla[l], True), wa2_pad, ba, gla_ng, gmean,
                          bs, ts_pad, ts_pad, ts)
        cb0 = jnp.pad(state_gdn_conv[l], ((0, 0), (8 - (CONV_K - 1), 0), (0, 0)))
        od, gdn_st = _gdn(rest_pad, cb0, _block_diag_expand(state_gdn[l], False), cw, alog_row, dtb_row, gdn_ng,
                          gsum, expb, expg, bs, ts_pad, ts_pad, ts)
        og = og.reshape(bs, ts_pad, GLA_V)[:, :ts].reshape(bs * ts, GLA_V)
        od = od.reshape(bs, ts_pad, GDN_V)[:, :ts].reshape(bs * ts, GDN_V)
        q4 = qn.reshape(bs, ts, 1, SB_HEADS, SB_DIM)
        eye = jnp.eye(SB_HEADS, dtype=BF16)[None, None, :, :, None]
        qbd = (q4 * eye).reshape(bs, ts * SB_HEADS, SB_W)
        bias_rows = jnp.broadcast_to(jnp.tile(bias, ts)[:, None], (ts * SB_HEADS, PAGE_SIZE)).astype(F32)
        kn = _pad_rows(sk, bs, ts, 8).reshape(bs, 8, SB_W)
        vn = _pad_rows(sv, bs, ts, 8).reshape(bs, 8, SB_W)
        osb = _sb_sample(page_table, qbd, bias_rows, kn, vn, mu, cache_k, cache_v, l)
        osb = osb.reshape(bs * ts, SB_W).astype(BF16)
        hs = _out_ffn(hs, og, od, osb, wo, ln2, wu, wd, bs * ts)
        xp = jnp.concatenate([state_gdn_conv[l], rest.reshape(bs, ts, R_WIDTH)[:, :, R_DQKV:R_DQKV + GDN_CONV_DIM]],
                             axis=1)
        conv_s = xp[:, xp.shape[1] - (CONV_K - 1):]
        outs_s.append((sk.reshape(bs, ts, SB_HEADS, SB_DIM), sv.reshape(bs, ts, SB_HEADS, SB_DIM),
                       _block_diag_extract(gla_st, GLA_HEADS, True),
                       _block_diag_extract(gdn_st, GDN_HEADS, False), conv_s))

    sbk_p, sbv_p, gla_p, gdn_p, conv_p = [jnp.stack(a) for a in zip(*outs_p)]
    sbk_s, sbv_s, gla_s, gdn_s, conv_s = [jnp.stack(a) for a in zip(*outs_s)]
    return (hp.reshape(bp, tp, d), hs.reshape(bs, ts, d), sbk_p, sbv_p, gla_p, gdn_p, conv_p,
            sbk_s, sbv_s, gla_s, gdn_s, conv_s)
```

```python
import functools
import math

import jax
import jax.numpy as jnp
from jax import lax
from jax.experimental import pallas as pl
from jax.experimental.pallas import tpu as pltpu

F32 = jnp.float32
BF16 = jnp.bfloat16
HI = lax.Precision.HIGHEST

EPS = 1e-6
HEAD_DIM = 64
GLA_HEADS = 4
GLA_DK = 32
GLA_DV = 64
GLA_RANK = 16
GLA_TAU = 16.0
GDN_HEADS = 4
GDN_DK = 64
GDN_DV = 64
CONV_K = 4
SB_HEADS = 8
SB_DIM = 64
PAGE_SIZE = 128
CHUNK = 64
SB_TQ = 1024
SB_TK = 512
SB_PAGES_PER_STEP = 16
SAMPLE_GROUP = 16

GLA_QK = GLA_HEADS * GLA_DK
GLA_V = GLA_HEADS * GLA_DV
GDN_QK = GDN_HEADS * GDN_DK
GDN_V = GDN_HEADS * GDN_DV
GDN_CONV_DIM = 2 * GDN_QK + GDN_V
SB_W = SB_HEADS * SB_DIM

R_DQKV = 0
R_GV = 768
R_GZ = 1024
R_DZ = 1280
R_GQ = 1536
R_GK = 1664
R_SMALL = 1792
R_WIDTH = 1920
LANE_LR = 0
LANE_DB = 16
LANE_DA = 20
N_PAD = 3 * SB_W + R_WIDTH

VMEM_LIMIT = 56 * 1024 * 1024
LOG2E = 1.4426950408889634
SB_Q_SCALE = LOG2E * SB_DIM ** -0.5


def _dot(a, b, precision=None):
    return jnp.dot(a, b, precision=precision, preferred_element_type=F32)


def _dot_nt(a, b, precision=None):
    return lax.dot_general(a, b, (((1,), (1,)), ((), ())), precision=precision, preferred_element_type=F32)


def _dot_tn(a, b, precision=None):
    return lax.dot_general(a, b, (((0,), (0,)), ((), ())), precision=precision, preferred_element_type=F32)


def _softplus(x):
    return jnp.maximum(x, 0.0) + jnp.log1p(jnp.exp(-jnp.abs(x)))


def _sigmoid(x):
    return 1.0 / (1.0 + jnp.exp(-x))


def _silu(x):
    return x * _sigmoid(x)


def _headnorm(x, g):
    outs = []
    lane = lax.broadcasted_iota(jnp.int32, (1, 128), 1)
    first = lane < HEAD_DIM
    for j in range(x.shape[1] // 128):
        xb = x[:, j * 128:(j + 1) * 128]
        sq = xb * xb
        s0 = jnp.sum(jnp.where(first, sq, 0.0), axis=-1, keepdims=True)
        s1 = jnp.sum(jnp.where(first, 0.0, sq), axis=-1, keepdims=True)
        ms = jnp.where(first, s0, s1) * (1.0 / HEAD_DIM)
        outs.append(xb * lax.rsqrt(ms + EPS))
    return jnp.concatenate(outs, axis=-1) * g


def _in_proj_kernel(x_ref, g_ref, w_ref, qg_ref, kg_ref, q_ref, k_ref, v_ref, r_ref):
    x = x_ref[...]
    ms = jnp.mean(x * x, axis=-1, keepdims=True)
    xn = (x * lax.rsqrt(ms + EPS) * g_ref[...]).astype(BF16)
    sq = _dot(xn, w_ref[:, 0:SB_W])
    q_ref[...] = (_headnorm(sq, qg_ref[...]) * SB_Q_SCALE).astype(BF16)
    sk = _dot(xn, w_ref[:, SB_W:2 * SB_W])
    k_ref[...] = _headnorm(sk, kg_ref[...])
    v_ref[...] = _dot(xn, w_ref[:, 2 * SB_W:3 * SB_W])
    for c0 in range(0, R_WIDTH, 640):
        r_ref[:, c0:c0 + 640] = _dot(xn, w_ref[:, 3 * SB_W + c0:3 * SB_W + c0 + 640])


def _in_proj(h, ln_g, w_pad, qn_g, kn_g, tm):
    m = h.shape[0]
    row = lambda i: (i, 0)
    const = lambda i: (0, 0)
    return pl.pallas_call(
        _in_proj_kernel,
        grid=(m // tm,),
        in_specs=[
            pl.BlockSpec((tm, h.shape[1]), row),
            pl.BlockSpec((1, h.shape[1]), const),
            pl.BlockSpec(w_pad.shape, const, pipeline_mode=pl.Buffered(1)),
            pl.BlockSpec((1, SB_W), const),
            pl.BlockSpec((1, SB_W), const),
        ],
        out_specs=[
            pl.BlockSpec((tm, SB_W), row),
            pl.BlockSpec((tm, SB_W), row),
            pl.BlockSpec((tm, SB_W), row),
            pl.BlockSpec((tm, R_WIDTH), row),
        ],
        out_shape=[
            jax.ShapeDtypeStruct((m, SB_W), BF16),
            jax.ShapeDtypeStruct((m, SB_W), F32),
            jax.ShapeDtypeStruct((m, SB_W), F32),
            jax.ShapeDtypeStruct((m, R_WIDTH), F32),
        ],
        compiler_params=pltpu.CompilerParams(dimension_semantics=("arbitrary",), vmem_limit_bytes=VMEM_LIMIT),
        name="in_proj",
    )(h, ln_g, w_pad, qn_g, kn_g)


def _out_ffn_kernel(h_ref, og_ref, od_ref, os_ref, wo_ref, g_ref, wu_ref, wd_ref, o_ref, *, f_chunk):
    mixed = jnp.concatenate([og_ref[...], od_ref[...], os_ref[...]], axis=-1)
    h = h_ref[...] + _dot(mixed, wo_ref[...])
    ms = jnp.mean(h * h, axis=-1, keepdims=True)
    hn = (h * lax.rsqrt(ms + EPS) * g_ref[...]).astype(BF16)
    d_ff = wu_ref.shape[1]
    acc = h
    for f0 in range(0, d_ff, f_chunk):
        u = _dot(hn, wu_ref[:, f0:f0 + f_chunk])
        u = jnp.maximum(u, 0.0)
        acc = acc + _dot((u * u).astype(BF16), wd_ref[f0:f0 + f_chunk, :])
    o_ref[...] = acc


def _out_ffn(h, og, od, osb, wo, ln_g, wu, wd, tm):
    m, d = h.shape
    row = lambda i: (i, 0)
    const = lambda i: (0, 0)
    return pl.pallas_call(
        functools.partial(_out_ffn_kernel, f_chunk=1024),
        grid=(m // tm,),
        in_specs=[
            pl.BlockSpec((tm, d), row),
            pl.BlockSpec((tm, GLA_V), row),
            pl.BlockSpec((tm, GDN_V), row),
            pl.BlockSpec((tm, SB_W), row),
            pl.BlockSpec(wo.shape, const, pipeline_mode=pl.Buffered(1)),
            pl.BlockSpec((1, d), const),
            pl.BlockSpec(wu.shape, const, pipeline_mode=pl.Buffered(1)),
            pl.BlockSpec(wd.shape, const, pipeline_mode=pl.Buffered(1)),
        ],
        out_specs=pl.BlockSpec((tm, d), row),
        out_shape=jax.ShapeDtypeStruct((m, d), F32),
        compiler_params=pltpu.CompilerParams(dimension_semantics=("arbitrary",), vmem_limit_bytes=VMEM_LIMIT),
        name="out_ffn",
    )(h, og, od, osb, wo, ln_g, wu, wd)


def _row_col(c):
    row = lax.broadcasted_iota(jnp.int32, (c, c), 0)
    col = lax.broadcasted_iota(jnp.int32, (c, c), 1)
    return row, col


def _gla_kernel(q_ref, k_ref, v_ref, z_ref, sm_ref, s0_ref, wa2_ref, ba_ref, ng_ref, gmean_ref,
                o_ref, sout_ref, st_scr, *, chunk, t_valid):
    c = chunk

    @pl.when(pl.program_id(1) == 0)
    def _():
        st_scr[...] = s0_ref[...]

    q = q_ref[...] * (GLA_DK ** -0.5)
    k = k_ref[...]
    v = v_ref[...]
    x = _dot(sm_ref[...], wa2_ref[...], HI) + ba_ref[...]
    log_a = -_softplus(-x) * (1.0 / GLA_TAU)
    if t_valid < c:
        valid = lax.broadcasted_iota(jnp.int32, (c, 1), 0) < t_valid
        log_a = jnp.where(valid, log_a, 0.0)
        q = jnp.where(valid, q, 0.0)
        k = jnp.where(valid, k, 0.0)
        v = jnp.where(valid, v, 0.0)
    row, col = _row_col(c)
    incl = row >= col
    b = _dot(incl.astype(F32), log_a, HI)
    b_last = b[c - 1:c, :]
    b_mid = b[c // 2 - 1:c // 2, :]
    q_in = (q * jnp.exp(b)).astype(BF16)
    q_t = q * jnp.exp(b - b_mid)
    k_t = (k * jnp.exp(b_mid - b)).astype(BF16)
    k_st = (k * jnp.exp(b_last - b)).astype(BF16)
    vb = v.astype(BF16)

    st = st_scr[...]
    o = _dot_nt(q_in, st.astype(BF16))
    lane_k = lax.broadcasted_iota(jnp.int32, (1, GLA_QK), 1) // GLA_DK
    lane_v = lax.broadcasted_iota(jnp.int32, (1, GLA_V), 1) // GLA_DV
    for h in range(GLA_HEADS):
        qh = jnp.where(lane_k == h, q_t, 0.0).astype(BF16)
        att = jnp.where(incl, _dot_nt(qh, k_t), 0.0).astype(BF16)
        o = o + _dot(att, jnp.where(lane_v == h, vb, jnp.zeros_like(vb)))

    bd = (lax.broadcasted_iota(jnp.int32, (GLA_V, GLA_QK), 0) // GLA_DV
          == lax.broadcasted_iota(jnp.int32, (GLA_V, GLA_QK), 1) // GLA_DK)
    st_new = st * jnp.exp(b_last) + jnp.where(bd, _dot_tn(vb, k_st), 0.0)
    st_scr[...] = st_new
    sout_ref[...] = st_new

    ms = _dot(o * o, gmean_ref[...], HI)
    y = o * lax.rsqrt(ms + EPS) * ng_ref[...]
    o_ref[...] = (y * _silu(z_ref[...])).astype(BF16)


def _gla(rest, s0t, wa2_pad, ba, ng_t, gmean, n_seq, seq_len, chunk, t_valid):
    nc = seq_len // chunk
    blk = lambda width, cb: pl.BlockSpec((chunk, width), lambda b, c: (b * nc + c, cb))
    const2 = lambda b, c: (0, 0)
    return pl.pallas_call(
        functools.partial(_gla_kernel, chunk=chunk, t_valid=t_valid),
        grid=(n_seq, nc),
        in_specs=[
            blk(GLA_QK, R_GQ // GLA_QK),
            blk(GLA_QK, R_GK // GLA_QK),
            blk(GLA_V, R_GV // GLA_V),
            blk(GLA_V, R_GZ // GLA_V),
            blk(128, R_SMALL // 128),
            pl.BlockSpec((None, GLA_V, GLA_QK), lambda b, c: (b, 0, 0)),
            pl.BlockSpec((128, GLA_QK), const2),
            pl.BlockSpec((1, GLA_QK), const2),
            pl.BlockSpec((1, GLA_V), const2),
            pl.BlockSpec((GLA_V, GLA_V), const2),
        ],
        out_specs=[
            pl.BlockSpec((chunk, GLA_V), lambda b, c: (b * nc + c, 0)),
            pl.BlockSpec((None, GLA_V, GLA_QK), lambda b, c: (b, 0, 0)),
        ],
        out_shape=[
            jax.ShapeDtypeStruct((n_seq * seq_len, GLA_V), BF16),
            jax.ShapeDtypeStruct((n_seq, GLA_V, GLA_QK), F32),
        ],
        scratch_shapes=[pltpu.VMEM((GLA_V, GLA_QK), F32)],
        compiler_params=pltpu.CompilerParams(dimension_semantics=("arbitrary", "arbitrary")),
        name="gla",
    )(rest, rest, rest, rest, rest, s0t, wa2_pad, ba, ng_t, gmean)


def _gdn_kernel(x_ref, sm_ref, z_ref, cb0_ref, s0_ref, cw_ref, alog_ref, dtb_ref, ng_ref, gsum_ref,
                expb_ref, expg_ref, o_ref, sout_ref, xp_scr, s_scr, *, chunk, t_valid):
    c = chunk

    @pl.when(pl.program_id(1) == 0)
    def _():
        xp_scr[0:8, :] = cb0_ref[...]
        s_scr[...] = s0_ref[...]

    xp_scr[8:8 + c, :] = x_ref[...]
    base = 8 - (CONV_K - 1)
    conv = xp_scr[base:base + c, :] * cw_ref[0:1, :]
    for i in range(1, CONV_K):
        conv = conv + xp_scr[base + i:base + i + c, :] * cw_ref[i:i + 1, :]
    xp_scr[0:8, :] = xp_scr[c:c + 8, :]
    conv = _silu(conv)
    cq = conv[:, 0:GDN_QK]
    ck = conv[:, GDN_QK:2 * GDN_QK]
    v = conv[:, 2 * GDN_QK:]
    gsum = gsum_ref[...]
    q = cq * lax.rsqrt(_dot(cq * cq, gsum, HI) + EPS) * (GDN_DK ** -0.5)
    k = ck * lax.rsqrt(_dot(ck * ck, gsum, HI) + EPS)

    sm = sm_ref[...]
    beta_all = _sigmoid(sm)
    g_all = -jnp.exp(alog_ref[...]) * _softplus(sm + dtb_ref[...])
    if t_valid < c:
        valid = lax.broadcasted_iota(jnp.int32, (c, 1), 0) < t_valid
        q = jnp.where(valid, q, 0.0)
        k = jnp.where(valid, k, 0.0)
        v = jnp.where(valid, v, 0.0)
        beta_all = jnp.where(valid, beta_all, 0.0)
        g_all = jnp.where(valid, g_all, 0.0)
    row, col = _row_col(c)
    incl = row >= col
    strict = row > col
    b_col = _dot(incl.astype(F32), g_all, HI)
    bfull = _dot(b_col, expg_ref[...], HI)
    beta_full = _dot(beta_all, expb_ref[...], HI)
    b_last = bfull[c - 1:c, :]
    eb = jnp.exp(bfull)
    kb = k * beta_full
    rv = v * beta_full
    rk = kb * eb
    q_in = (q * eb).astype(BF16)
    k_st = (k * jnp.exp(b_last - bfull)).astype(BF16)
    kbf = k.astype(BF16)

    lane_h = lax.broadcasted_iota(jnp.int32, (1, GDN_V), 1) // GDN_DV
    lane_s = lax.broadcasted_iota(jnp.int32, (c, 128), 1)
    eye = (row == col).astype(F32)
    n_sq = int(round(math.log2(c))) - 1
    u = jnp.zeros((c, GDN_V), F32)
    w = jnp.zeros((c, GDN_V), F32)
    atts = []
    for h in range(GDN_HEADS):
        mh = lane_h == h
        sel = (lane_s == LANE_DA + h).astype(F32)
        b_s = _dot_nt(sel, b_col, HI)
        b_t = jnp.broadcast_to(b_col[:, LANE_DA + h:LANE_DA + h + 1], (c, c))
        dec = jnp.exp(jnp.minimum(b_t - b_s, 0.0))
        a = jnp.where(strict, _dot_nt(jnp.where(mh, kb, 0.0), k, HI) * dec, 0.0)
        p = eye - a
        qq = a
        for _ in range(n_sq):
            qq = _dot(qq, qq, HI)
            p = p + _dot(p, qq, HI)
        u = u + _dot(p, jnp.where(mh, rv, 0.0), HI)
        w = w + _dot(p, jnp.where(mh, rk, 0.0), HI)
        att = _dot_nt(jnp.where(mh, q, 0.0).astype(BF16), kbf)
        atts.append(jnp.where(incl, att * dec, 0.0).astype(BF16))

    s = s_scr[...]
    sb = s.astype(BF16)
    v_new = u - _dot(w.astype(BF16), sb)
    vnb = v_new.astype(BF16)
    o = _dot(q_in, sb)
    for h in range(GDN_HEADS):
        o = o + _dot(atts[h], jnp.where(lane_h == h, vnb, jnp.zeros_like(vnb)))
    bd = (lax.broadcasted_iota(jnp.int32, (GDN_QK, GDN_V), 0) // GDN_DK
          == lax.broadcasted_iota(jnp.int32, (GDN_QK, GDN_V), 1) // GDN_DV)
    s_new = s * jnp.exp(b_last) + jnp.where(bd, _dot_tn(k_st, vnb), 0.0)
    s_scr[...] = s_new
    sout_ref[...] = s_new

    gmean = gsum * (1.0 / GDN_DV)
    ms = _dot(o * o, gmean, HI)
    y = o * lax.rsqrt(ms + EPS) * ng_ref[...]
    o_ref[...] = (y * _silu(z_ref[...])).astype(BF16)


def _gdn(rest, cb0, s0, cw, alog_row, dtb_row, ng_t, gsum, expb, expg, n_seq, seq_len, chunk, t_valid):
    nc = seq_len // chunk
    blk = lambda width, cb: pl.BlockSpec((chunk, width), lambda b, c: (b * nc + c, cb))
    const2 = lambda b, c: (0, 0)
    return pl.pallas_call(
        functools.partial(_gdn_kernel, chunk=chunk, t_valid=t_valid),
        grid=(n_seq, nc),
        in_specs=[
            blk(GDN_CONV_DIM, R_DQKV // GDN_CONV_DIM),
            blk(128, R_SMALL // 128),
            blk(GDN_V, R_DZ // GDN_V),
            pl.BlockSpec((None, 8, GDN_CONV_DIM), lambda b, c: (b, 0, 0)),
            pl.BlockSpec((None, GDN_QK, GDN_V), lambda b, c: (b, 0, 0)),
            pl.BlockSpec((CONV_K, GDN_CONV_DIM), const2),
            pl.BlockSpec((1, 128), const2),
            pl.BlockSpec((1, 128), const2),
            pl.BlockSpec((1, GDN_V), const2),
            pl.BlockSpec((GDN_V, GDN_V), const2),
            pl.BlockSpec((128, GDN_V), const2),
            pl.BlockSpec((128, GDN_V), const2),
        ],
        out_specs=[
            pl.BlockSpec((chunk, GDN_V), lambda b, c: (b * nc + c, 0)),
            pl.BlockSpec((None, GDN_QK, GDN_V), lambda b, c: (b, 0, 0)),
        ],
        out_shape=[
            jax.ShapeDtypeStruct((n_seq * seq_len, GDN_V), BF16),
            jax.ShapeDtypeStruct((n_seq, GDN_QK, GDN_V), F32),
        ],
        scratch_shapes=[pltpu.VMEM((chunk + 8, GDN_CONV_DIM), F32), pltpu.VMEM((GDN_QK, GDN_V), F32)],
        compiler_params=pltpu.CompilerParams(dimension_semantics=("arbitrary", "arbitrary")),
        name="gdn",
    )(rest, rest, rest, cb0, s0, cw, alog_row, dtb_row, ng_t, gsum, expb, expg)


def _split(x):
    hi = x.astype(BF16)
    return hi, (x - hi.astype(F32)).astype(BF16)


def _dot_x3(a, b):
    ah, al = _split(a)
    bh, bl = _split(b)
    return _dot(ah, bh) + (_dot(al, bh) + _dot(ah, bl))


def _dot_x2(a, b_exact):
    ah, al = _split(a)
    return _dot(ah, b_exact) + _dot(al, b_exact)


def _per_head_lanes(cols, first_lane, heads, width):
    c = cols.shape[0]
    lane_h = lax.broadcasted_iota(jnp.int32, (1, heads * width), 1) // width
    out = jnp.broadcast_to(cols[:, first_lane:first_lane + 1], (c, heads * width))
    for h in range(1, heads):
        out = jnp.where(lane_h == h, jnp.broadcast_to(cols[:, first_lane + h:first_lane + h + 1], (c, heads * width)), out)
    return out


def _gla_chunk(q, k, v, z, sm, st, wa2, ba, ng, gmean_b, *, c, t_valid):
    q = q * (GLA_DK ** -0.5)
    x = _dot_x3(sm, wa2) + ba
    log_a = -_softplus(-x) * (1.0 / GLA_TAU)
    if t_valid < c:
        valid = lax.broadcasted_iota(jnp.int32, (c, 1), 0) < t_valid
        log_a = jnp.where(valid, log_a, 0.0)
        q = jnp.where(valid, q, 0.0)
        k = jnp.where(valid, k, 0.0)
        v = jnp.where(valid, v, 0.0)
    row, col = _row_col(c)
    incl = row >= col
    b = _dot_x2_lhs_exact(incl.astype(BF16), log_a)
    b_last = b[c - 1:c, :]
    b_mid = b[c // 2 - 1:c // 2, :]
    q_in = (q * jnp.exp(b)).astype(BF16)
    q_t = q * jnp.exp(b - b_mid)
    k_t = (k * jnp.exp(b_mid - b)).astype(BF16)
    k_st = (k * jnp.exp(b_last - b)).astype(BF16)
    vb = v.astype(BF16)

    o = _dot_nt(q_in, st.astype(BF16))
    lane_k = lax.broadcasted_iota(jnp.int32, (1, GLA_QK), 1) // GLA_DK
    lane_v = lax.broadcasted_iota(jnp.int32, (1, GLA_V), 1) // GLA_DV
    for h in range(GLA_HEADS):
        qh = jnp.where(lane_k == h, q_t, 0.0).astype(BF16)
        att = jnp.where(incl, _dot_nt(qh, k_t), 0.0).astype(BF16)
        o = o + _dot(att, jnp.where(lane_v == h, vb, jnp.zeros_like(vb)))

    bd = (lax.broadcasted_iota(jnp.int32, (GLA_V, GLA_QK), 0) // GLA_DV
          == lax.broadcasted_iota(jnp.int32, (GLA_V, GLA_QK), 1) // GLA_DK)
    st_new = st * jnp.exp(b_last) + jnp.where(bd, _dot_tn(vb, k_st), 0.0)
    ms = _dot_x2(o * o, gmean_b) * (1.0 / GLA_DV)
    y = o * lax.rsqrt(ms + EPS) * ng
    return (y * _silu(z)).astype(BF16), st_new


def _dot_x2_lhs_exact(a_exact, b):
    bh, bl = _split(b)
    return _dot(a_exact, bh) + _dot(a_exact, bl)


def _gla_group_kernel(q_ref, k_ref, v_ref, z_ref, sm_ref, s0_ref, wa2_ref, ba_ref, ng_ref, gsum_ref,
                      o_ref, sout_ref, st_scr, *, group, chunk, t_valid):
    @pl.when(pl.program_id(1) == 0)
    def _():
        st_scr[...] = s0_ref[...]

    for g in range(group):
        y, st_new = _gla_chunk(q_ref[g], k_ref[g], v_ref[g], z_ref[g], sm_ref[g], st_scr[g], wa2_ref[...],
                               ba_ref[...], ng_ref[...], gsum_ref[...], c=chunk, t_valid=t_valid)
        o_ref[g] = y
        st_scr[g] = st_new
        sout_ref[g] = st_new


def _gla_group(rest3, s0t, wa2_pad, ba, ng_t, gsum_b, group, chunk, t_valid):
    n_seq, seq_len, _ = rest3.shape
    nc = seq_len // chunk
    blk = lambda width, cb: pl.BlockSpec((group, chunk, width), lambda b, c: (b, c, cb))
    const2 = lambda b, c: (0, 0)
    return pl.pallas_call(
        functools.partial(_gla_group_kernel, group=group, chunk=chunk, t_valid=t_valid),
        grid=(n_seq // group, nc),
        in_specs=[
            blk(GLA_QK, R_GQ // GLA_QK),
            blk(GLA_QK, R_GK // GLA_QK),
            blk(GLA_V, R_GV // GLA_V),
            blk(GLA_V, R_GZ // GLA_V),
            blk(128, R_SMALL // 128),
            pl.BlockSpec((group, GLA_V, GLA_QK), lambda b, c: (b, 0, 0)),
            pl.BlockSpec((128, GLA_QK), const2),
            pl.BlockSpec((1, GLA_QK), const2),
            pl.BlockSpec((1, GLA_V), const2),
            pl.BlockSpec((GLA_V, GLA_V), const2),
        ],
        out_specs=[
            pl.BlockSpec((group, chunk, GLA_V), lambda b, c: (b, c, 0)),
            pl.BlockSpec((group, GLA_V, GLA_QK), lambda b, c: (b, 0, 0)),
        ],
        out_shape=[
            jax.ShapeDtypeStruct((n_seq, seq_len, GLA_V), BF16),
            jax.ShapeDtypeStruct((n_seq, GLA_V, GLA_QK), F32),
        ],
        scratch_shapes=[pltpu.VMEM((group, GLA_V, GLA_QK), F32)],
        compiler_params=pltpu.CompilerParams(dimension_semantics=("arbitrary", "arbitrary"),
                                             vmem_limit_bytes=VMEM_LIMIT),
        name="gla",
    )(rest3, rest3, rest3, rest3, rest3, s0t, wa2_pad, ba, ng_t, gsum_b)


def _gdn_pre(xp_ref, x, sm, cw, alog, dtb, gsum_b, *, c, t_valid):
    xp_ref[8:8 + c, :] = x
    base = 8 - (CONV_K - 1)
    conv = xp_ref[base:base + c, :] * cw[0:1, :]
    for i in range(1, CONV_K):
        conv = conv + xp_ref[base + i:base + i + c, :] * cw[i:i + 1, :]
    xp_ref[0:8, :] = xp_ref[c:c + 8, :]
    conv = _silu(conv)
    cq = conv[:, 0:GDN_QK]
    ck = conv[:, GDN_QK:2 * GDN_QK]
    v = conv[:, 2 * GDN_QK:]
    q = cq * lax.rsqrt(_dot_x2(cq * cq, gsum_b) + EPS) * (GDN_DK ** -0.5)
    k = ck * lax.rsqrt(_dot_x2(ck * ck, gsum_b) + EPS)

    beta_all = _sigmoid(sm)
    g_all = -jnp.exp(alog) * _softplus(sm + dtb)
    if t_valid < c:
        valid = lax.broadcasted_iota(jnp.int32, (c, 1), 0) < t_valid
        q = jnp.where(valid, q, 0.0)
        k = jnp.where(valid, k, 0.0)
        v = jnp.where(valid, v, 0.0)
        beta_all = jnp.where(valid, beta_all, 0.0)
        g_all = jnp.where(valid, g_all, 0.0)
    row, col = _row_col(c)
    b_col = _dot_x2_lhs_exact((row >= col).astype(BF16), g_all)
    bfull = _per_head_lanes(b_col, LANE_DA, GDN_HEADS, GDN_DV)
    beta_full = _per_head_lanes(beta_all, LANE_DB, GDN_HEADS, GDN_DV)
    b_last = bfull[c - 1:c, :]
    eb = jnp.exp(bfull)
    kb = k * beta_full
    return dict(
        b_col=b_col, b_row=b_col.T, b_last=b_last,
        q_in=(q * eb).astype(BF16), k_st=(k * jnp.exp(b_last - bfull)).astype(BF16),
        kbf=k.astype(BF16), kbb=kb.astype(BF16), qb=q.astype(BF16),
        rhs=jnp.concatenate([v * beta_full, kb * eb], axis=1).astype(BF16))


def _gdn_post(pre, uw, atts, s, z, ng, gsum_b):
    lane_h = lax.broadcasted_iota(jnp.int32, (1, GDN_V), 1) // GDN_DV
    sb = s.astype(BF16)
    v_new = uw[:, :GDN_V] - _dot(uw[:, GDN_V:].astype(BF16), sb)
    vnb = v_new.astype(BF16)
    o = _dot(pre["q_in"], sb)
    for h in range(GDN_HEADS):
        o = o + _dot(atts[h], jnp.where(lane_h == h, vnb, jnp.zeros_like(vnb)))
    bd = (lax.broadcasted_iota(jnp.int32, (GDN_QK, GDN_V), 0) // GDN_DK
          == lax.broadcasted_iota(jnp.int32, (GDN_QK, GDN_V), 1) // GDN_DV)
    s_new = s * jnp.exp(pre["b_last"]) + jnp.where(bd, _dot_tn(pre["k_st"], vnb), 0.0)
    ms = _dot_x2(o * o, gsum_b) * (1.0 / GDN_DV)
    y = o * lax.rsqrt(ms + EPS) * ng
    return (y * _silu(z)).astype(BF16), s_new


def _gdn_group_kernel(x_ref, sm_ref, z_ref, cb0_ref, s0_ref, cw_ref, alog_ref, dtb_ref, ng_ref, gsum_ref,
                      o_ref, sout_ref, xp_scr, s_scr, *, group, chunk, t_valid):
    c = chunk

    @pl.when(pl.program_id(1) == 0)
    def _():
        xp_scr[:, 0:8, :] = cb0_ref[...]
        s_scr[...] = s0_ref[...]

    gsum_b = gsum_ref[...]
    pre = [_gdn_pre(xp_scr.at[g], x_ref[g], sm_ref[g], cw_ref[...], alog_ref[...], dtb_ref[...], gsum_b,
                    c=c, t_valid=t_valid) for g in range(group)]

    row, col = _row_col(c)
    incl = row >= col
    strict = row > col
    eye = (row == col).astype(F32)
    lane_h = lax.broadcasted_iota(jnp.int32, (1, GDN_V), 1) // GDN_DV
    lane_h2 = lax.broadcasted_iota(jnp.int32, (1, 2 * GDN_V), 1) % GDN_V // GDN_DV
    chains = [(g, h) for g in range(group) for h in range(GDN_HEADS)]
    dec, qq, p, atts = {}, {}, {}, {}
    for g, h in chains:
        pg = pre[g]
        b_t = jnp.broadcast_to(pg["b_col"][:, LANE_DA + h:LANE_DA + h + 1], (c, c))
        b_s = jnp.broadcast_to(pg["b_row"][LANE_DA + h:LANE_DA + h + 1, :], (c, c))
        dec[g, h] = jnp.exp(jnp.minimum(b_t - b_s, 0.0))
    for g, h in chains:
        pg = pre[g]
        kbh = jnp.where(lane_h == h, pg["kbb"], jnp.zeros_like(pg["kbb"]))
        a = jnp.where(strict, _dot_nt(kbh, pg["kbf"]) * dec[g, h], 0.0)
        qq[g, h] = a
        p[g, h] = eye - a
    for _ in range(int(round(math.log2(c))) - 1):
        for ch in chains:
            qq[ch] = _dot_x3(qq[ch], qq[ch])
        for ch in chains:
            p[ch] = p[ch] + _dot_x3(p[ch], qq[ch])
    for g, h in chains:
        pg = pre[g]
        qh = jnp.where(lane_h == h, pg["qb"], jnp.zeros_like(pg["qb"]))
        atts[g, h] = jnp.where(incl, _dot_nt(qh, pg["kbf"]) * dec[g, h], 0.0).astype(BF16)
    uws = []
    for g in range(group):
        rhs = pre[g]["rhs"]
        uw = _dot(p[g, 0].astype(BF16), jnp.where(lane_h2 == 0, rhs, jnp.zeros_like(rhs)))
        for h in range(1, GDN_HEADS):
            uw = uw + _dot(p[g, h].astype(BF16), jnp.where(lane_h2 == h, rhs, jnp.zeros_like(rhs)))
        uws.append(uw)

    for g in range(group):
        y, s_new = _gdn_post(pre[g], uws[g], [atts[g, h] for h in range(GDN_HEADS)], s_scr[g], z_ref[g],
                             ng_ref[...], gsum_b)
        o_ref[g] = y
        s_scr[g] = s_new
        sout_ref[g] = s_new


def _gdn_group(rest3, cb0, s0, cw, alog_row, dtb_row, ng_t, gsum_b, group, chunk, t_valid):
    n_seq, seq_len, _ = rest3.shape
    nc = seq_len // chunk
    blk = lambda width, cb: pl.BlockSpec((group, chunk, width), lambda b, c: (b, c, cb))
    const2 = lambda b, c: (0, 0)
    return pl.pallas_call(
        functools.partial(_gdn_group_kernel, group=group, chunk=chunk, t_valid=t_valid),
        grid=(n_seq // group, nc),
        in_specs=[
            blk(GDN_CONV_DIM, R_DQKV // GDN_CONV_DIM),
            blk(128, R_SMALL // 128),
            blk(GDN_V, R_DZ // GDN_V),
            pl.BlockSpec((group, 8, GDN_CONV_DIM), lambda b, c: (b, 0, 0)),
            pl.BlockSpec((group, GDN_QK, GDN_V), lambda b, c: (b, 0, 0)),
            pl.BlockSpec((CONV_K, GDN_CONV_DIM), const2),
            pl.BlockSpec((1, 128), const2),
            pl.BlockSpec((1, 128), const2),
            pl.BlockSpec((1, GDN_V), const2),
            pl.BlockSpec((GDN_V, GDN_V), const2),
        ],
        out_specs=[
            pl.BlockSpec((group, chunk, GDN_V), lambda b, c: (b, c, 0)),
            pl.BlockSpec((group, GDN_QK, GDN_V), lambda b, c: (b, 0, 0)),
        ],
        out_shape=[
            jax.ShapeDtypeStruct((n_seq, seq_len, GDN_V), BF16),
            jax.ShapeDtypeStruct((n_seq, GDN_QK, GDN_V), F32),
        ],
        scratch_shapes=[pltpu.VMEM((group, chunk + 8, GDN_CONV_DIM), F32), pltpu.VMEM((group, GDN_QK, GDN_V), F32)],
        compiler_params=pltpu.CompilerParams(dimension_semantics=("arbitrary", "arbitrary"),
                                             vmem_limit_bytes=VMEM_LIMIT),
        name="gdn",
    )(rest3, rest3, rest3, cb0, s0, cw, alog_row, dtb_row, ng_t, gsum_b)


def _sb_tile(z, c, mask, mu):
    tk = z.shape[1]
    l = jnp.log2(1.0 + jnp.exp2(-jnp.abs(z)))
    log_beta = jnp.minimum(z, 0.0) - l
    log_1m = log_beta - z
    if mask is not None:
        log_1m = jnp.where(mask, log_1m, 0.0)
    l16 = log_1m.astype(BF16)
    xs = []
    for s in range(tk // 128 - 1, -1, -1):
        sc = _dot(l16[:, s * 128:(s + 1) * 128], mu)
        xs.append(log_beta[:, s * 128:(s + 1) * 128] + sc[:, :128] + c)
        c = c + sc[:, 128:]
    x = xs[0] if len(xs) == 1 else jnp.concatenate(xs[::-1], axis=1)
    att = jnp.exp2(x)
    if mask is not None:
        att = jnp.where(mask, att, 0.0)
    return att.astype(BF16), c


def _sb_prompt_kernel(bias_ref, q_ref, k_ref, v_ref, mu_ref, o_ref, kb_scr, vb_scr, c_scr, o_scr, *, tq, tk):
    hp = pl.program_id(1)
    i = pl.program_id(2)

    @pl.when(i == 0)
    def _():
        kb_scr[...] = k_ref[...].astype(BF16)
        vb_scr[...] = v_ref[...].astype(BF16)

    q = q_ref[...]
    lane = lax.broadcasted_iota(jnp.int32, (1, 2 * SB_DIM), 1)
    first = lane < SB_DIM
    zero = jnp.zeros_like(q)
    qs = (jnp.where(first, q, zero), jnp.where(first, zero, q))
    bias = (bias_ref[2 * hp], bias_ref[2 * hp + 1])
    mu = mu_ref[...]
    c_scr[...] = jnp.zeros_like(c_scr)
    o_scr[...] = jnp.zeros_like(o_scr)
    n_sub = tq // tk

    def tile(j, masked, r0):
        start = pl.multiple_of(j * tk, tk)
        kblk = kb_scr[pl.ds(start, tk), :]
        vblk = vb_scr[pl.ds(start, tk), :]
        mask = None
        if masked:
            q_pos = i * tq + r0 + lax.broadcasted_iota(jnp.int32, (tq - r0, tk), 0)
            mask = start + lax.broadcasted_iota(jnp.int32, (tq - r0, tk), 1) < q_pos
        for hh in range(2):
            z = _dot_nt(qs[hh][r0:], kblk) + bias[hh]
            att, c = _sb_tile(z, c_scr[hh, r0:], mask, mu)
            c_scr[hh, r0:] = c
            o_scr[hh, r0:] = o_scr[hh, r0:] + _dot(att, vblk)

    for dd in range(n_sub - 1, -1, -1):
        tile(i * n_sub + dd, True, dd * tk)

    def body(jj, carry):
        tile(i * n_sub - 1 - jj, False, 0)
        return carry

    lax.fori_loop(0, i * n_sub, body, 0)
    o_ref[...] = jnp.where(first, o_scr[0], o_scr[1]).astype(BF16)


def _sb_prompt(qn, sk, sv, bias, mu, n_seq, seq_len, tq, tk):
    nq = seq_len // tq
    n_hp = SB_HEADS // 2
    grid_spec = pltpu.PrefetchScalarGridSpec(
        num_scalar_prefetch=1,
        grid=(n_seq, n_hp, nq),
        in_specs=[
            pl.BlockSpec((tq, 2 * SB_DIM), lambda b, hp, i, bias: (b * nq + i, hp)),
            pl.BlockSpec((seq_len, 2 * SB_DIM), lambda b, hp, i, bias: (b, hp)),
            pl.BlockSpec((seq_len, 2 * SB_DIM), lambda b, hp, i, bias: (b, hp)),
            pl.BlockSpec((128, 256), lambda b, hp, i, bias: (0, 0)),
        ],
        out_specs=pl.BlockSpec((tq, 2 * SB_DIM), lambda b, hp, i, bias: (b * nq + i, hp)),
        scratch_shapes=[pltpu.VMEM((seq_len, 2 * SB_DIM), BF16), pltpu.VMEM((seq_len, 2 * SB_DIM), BF16),
                        pltpu.VMEM((2, tq, 128), F32), pltpu.VMEM((2, tq, 2 * SB_DIM), F32)],
    )
    return pl.pallas_call(
        functools.partial(_sb_prompt_kernel, tq=tq, tk=tk),
        grid_spec=grid_spec,
        out_shape=jax.ShapeDtypeStruct((n_seq * seq_len, SB_W), BF16),
        compiler_params=pltpu.CompilerParams(dimension_semantics=("arbitrary", "arbitrary", "arbitrary"),
                                             vmem_limit_bytes=VMEM_LIMIT),
        name="sb_prompt",
    )(bias, qn, sk, sv, mu)


def _sb_sample_kernel(pt_ref, q_ref, brow_ref, kn_ref, vn_ref, mu_ref, *rest, n_op, t_new):
    k_refs = rest[:n_op]
    v_refs = rest[n_op:2 * n_op]
    o_ref = rest[2 * n_op]
    c_scr, acc_scr = rest[2 * n_op + 1:]
    del pt_ref
    step = pl.program_id(1)
    q = q_ref[...]
    rows = q.shape[0]
    page_rows = PAGE_SIZE * SB_HEADS
    mu = mu_ref[...]
    bias = jnp.concatenate([brow_ref[...]] * (page_rows // 128), axis=1)
    h_of_row = lax.broadcasted_iota(jnp.int32, (rows, page_rows), 0) % SB_HEADS
    lane = lax.broadcasted_iota(jnp.int32, (rows, page_rows), 1)
    valid = lane % SB_HEADS == h_of_row

    def page(kp, vp, mask, c, acc):
        att, c = _sb_tile(_dot_nt(q, kp) + bias[:, :kp.shape[0]], c, mask, mu)
        return c, acc + _dot(att, vp)

    @pl.when(step == 0)
    def _():
        n_new = kn_ref.shape[0]
        pad = jnp.zeros((128 - n_new, SB_DIM), BF16)
        kn = jnp.concatenate([kn_ref[...].astype(BF16), pad], axis=0)
        vn = jnp.concatenate([vn_ref[...].astype(BF16), pad], axis=0)
        row_n = lax.broadcasted_iota(jnp.int32, (rows, 128), 0)
        lane_n = lax.broadcasted_iota(jnp.int32, (rows, 128), 1)
        mask = (lane_n % SB_HEADS == row_n % SB_HEADS) & (lane_n // SB_HEADS < row_n // SB_HEADS)
        c, acc = page(kn, vn, mask, jnp.zeros((rows, 128), F32), jnp.zeros((rows, SB_DIM), F32))
        c_scr[...] = c
        acc_scr[...] = acc

    c = c_scr[...]
    acc = acc_scr[...]
    for p in range(n_op - 1, -1, -1):
        kp = k_refs[p][...].reshape(page_rows, SB_DIM).astype(BF16)
        vp = v_refs[p][...].reshape(page_rows, SB_DIM).astype(BF16)
        c, acc = page(kp, vp, valid, c, acc)
    c_scr[...] = c
    acc_scr[...] = acc
    o_ref[...] = acc


def _sb_sample(page_table, q, bias_rows, kn, vn, mu, cache_k, cache_v, layer, n_op):
    n_b, n_pages = page_table.shape
    rows = q.shape[1]
    t_new = rows // SB_HEADS
    n_steps = n_pages // n_op

    def page_spec(p):
        return pl.BlockSpec((None, None, PAGE_SIZE, SB_HEADS, SB_DIM),
                            lambda b, s, pt: (layer, pt[b, (n_steps - 1 - s) * n_op + p], 0, 0, 0))

    grid_spec = pltpu.PrefetchScalarGridSpec(
        num_scalar_prefetch=1,
        grid=(n_b, n_steps),
        in_specs=[
            pl.BlockSpec((None, rows, SB_DIM), lambda b, s, pt: (b, 0, 0)),
            pl.BlockSpec((rows, 128), lambda b, s, pt: (0, 0)),
            pl.BlockSpec((None, rows, SB_DIM), lambda b, s, pt: (b, 0, 0)),
            pl.BlockSpec((None, rows, SB_DIM), lambda b, s, pt: (b, 0, 0)),
            pl.BlockSpec((128, 256), lambda b, s, pt: (0, 0)),
        ] + [page_spec(p) for p in range(n_op)] + [page_spec(p) for p in range(n_op)],
        out_specs=pl.BlockSpec((None, rows, SB_DIM), lambda b, s, pt: (b, 0, 0)),
        scratch_shapes=[pltpu.VMEM((rows, 128), F32), pltpu.VMEM((rows, SB_DIM), F32)],
    )
    return pl.pallas_call(
        functools.partial(_sb_sample_kernel, n_op=n_op, t_new=t_new),
        grid_spec=grid_spec,
        out_shape=jax.ShapeDtypeStruct((n_b, rows, SB_DIM), F32),
        compiler_params=pltpu.CompilerParams(dimension_semantics=("arbitrary", "arbitrary"),
                                             vmem_limit_bytes=VMEM_LIMIT),
        name="sb_sample",
    )(page_table, q, bias_rows, kn, vn, mu, *([cache_k] * n_op), *([cache_v] * n_op))


def _regroup_w_in(w_in):
    sizes = (GLA_QK, GLA_QK, GLA_V, GLA_RANK, GLA_V, GDN_CONV_DIM, GDN_HEADS, GDN_HEADS, GDN_V, SB_W, SB_W, SB_W)
    offs = [0]
    for s in sizes:
        offs.append(offs[-1] + s)
    (g_q, g_k, g_v, g_lr, g_z, d_qkv, d_b, d_a, d_z, s_q, s_k, s_v) = [
        w_in[:, offs[i]:offs[i + 1]] for i in range(len(sizes))]
    d = w_in.shape[0]
    small = jnp.concatenate([g_lr, d_b, d_a, jnp.zeros((d, 128 - GLA_RANK - 2 * GDN_HEADS), w_in.dtype)], axis=1)
    return jnp.concatenate([s_q, s_k, s_v, d_qkv, g_v, g_z, d_z, g_q, g_k, small], axis=1).astype(BF16)


def _block_diag_expand(s, transpose):
    bsz, h, a, b = s.shape
    if transpose:
        s = jnp.swapaxes(s, 2, 3)
        a, b = b, a
    eye = jnp.eye(h, dtype=s.dtype)
    return (s[:, :, :, None, :] * eye[None, :, None, :, None]).reshape(bsz, h * a, h * b)


def _block_diag_extract(s, h, transpose):
    bsz, ra, rb = s.shape
    a, b = ra // h, rb // h
    s = s.reshape(bsz, h, a, h, b)
    s = jnp.stack([s[:, i, :, i, :] for i in range(h)], axis=1)
    return jnp.swapaxes(s, 2, 3) if transpose else s


def _group_matrix(n, group, scale):
    idx = jnp.arange(n) // group
    return (idx[:, None] == idx[None, :]).astype(F32) * scale


def _lane_expand(first_lane, heads, width):
    src = jnp.arange(128)[:, None]
    dst_h = jnp.arange(heads * width)[None, :] // width
    return (src == first_lane + dst_h).astype(F32)


def _suffix_matrix(tk):
    j = jnp.arange(tk)[:, None]
    s = jnp.arange(tk)[None, :]
    return jnp.concatenate([(j > s), jnp.ones((tk, tk), bool)], axis=1).astype(BF16)


def _pad_rows(a, n_seq, t, t_pad):
    w = a.shape[-1]
    a = a.reshape(n_seq, t, w)
    return jnp.pad(a, ((0, 0), (0, t_pad - t), (0, 0))).reshape(n_seq * t_pad, w)


def kernel(x_prompt, x_sample, cache_sb_k, cache_sb_v, page_table, state_gla, state_gdn, state_gdn_conv,
           ln1_g, w_in, gla_wa2, gla_ba, gla_norm_g, gdn_conv_w, gdn_a_log, gdn_dt_bias, gdn_norm_g,
           sb_q_norm_g, sb_k_norm_g, sb_logit_bias, w_out, ln2_g, w_up, w_down):
    depth = w_in.shape[0]
    bp, tp, d = x_prompt.shape
    bs, ts, _ = x_sample.shape
    ts_pad = 8
    assert ts <= ts_pad and ts >= CONV_K - 1 and tp % CHUNK == 0
    pages_per_step = min(SB_PAGES_PER_STEP, page_table.shape[1])
    assert page_table.shape[1] % pages_per_step == 0 and ts * SB_HEADS <= 128

    hp = x_prompt.reshape(bp * tp, d)
    hs = x_sample.reshape(bs * ts, d)

    gsum_b = _group_matrix(GDN_V, HEAD_DIM, 1.0).astype(BF16)
    mu = _suffix_matrix(PAGE_SIZE)
    sample_group = math.gcd(bs, SAMPLE_GROUP)
    zeros_gla_p = jnp.zeros((bp, GLA_V, GLA_QK), F32)
    zeros_gdn_p = jnp.zeros((bp, GDN_QK, GDN_V), F32)
    zeros_cb_p = jnp.zeros((bp, 8, GDN_CONV_DIM), F32)

    outs_p, outs_s = [], []
    for l in range(depth):
        w_pad = _regroup_w_in(w_in[l])
        wo = w_out[l].astype(BF16)
        wu = w_up[l].astype(BF16)
        wd = w_down[l].astype(BF16)
        ln1 = ln1_g[l][None, :]
        ln2 = ln2_g[l][None, :]
        qg = jnp.tile(sb_q_norm_g[l], SB_HEADS)[None, :]
        kg = jnp.tile(sb_k_norm_g[l], SB_HEADS)[None, :]
        wa2_pad = jnp.zeros((128, GLA_QK), F32).at[LANE_LR:LANE_LR + GLA_RANK].set(gla_wa2[l])
        ba = gla_ba[l][None, :]
        gla_ng = jnp.tile(gla_norm_g[l], GLA_HEADS)[None, :]
        gdn_ng = jnp.tile(gdn_norm_g[l], GDN_HEADS)[None, :]
        alog_row = jnp.zeros((1, 128), F32).at[0, LANE_DA:LANE_DA + GDN_HEADS].set(gdn_a_log[l])
        dtb_row = jnp.zeros((1, 128), F32).at[0, LANE_DA:LANE_DA + GDN_HEADS].set(gdn_dt_bias[l])
        cw = gdn_conv_w[l]
        bias = sb_logit_bias[l] * LOG2E

        qn, sk, sv, rest = _in_proj(hp, ln1, w_pad, qg, kg, 512)
        rest3 = rest.reshape(bp, tp, R_WIDTH)
        og, gla_st = _gla_group(rest3, zeros_gla_p, wa2_pad, ba, gla_ng, gsum_b, bp, CHUNK, CHUNK)
        od, gdn_st = _gdn_group(rest3, zeros_cb_p, zeros_gdn_p, cw, alog_row, dtb_row, gdn_ng, gsum_b,
                                bp, CHUNK, CHUNK)
        og = og.reshape(bp * tp, GLA_V)
        od = od.reshape(bp * tp, GDN_V)
        osb = _sb_prompt(qn, sk, sv, bias, mu, bp, tp, min(SB_TQ, tp), min(SB_TK, tp))
        hp = _out_ffn(hp, og, od, osb, wo, ln2, wu, wd, 512)
        conv_p = rest.reshape(bp, tp, R_WIDTH)[:, tp - (CONV_K - 1):, R_DQKV:R_DQKV + GDN_CONV_DIM]
        outs_p.append((sk.reshape(bp, tp, SB_HEADS, SB_DIM), sv.reshape(bp, tp, SB_HEADS, SB_DIM),
                       _block_diag_extract(gla_st, GLA_HEADS, True),
                       _block_diag_extract(gdn_st, GDN_HEADS, False), conv_p))

        qn, sk, sv, rest = _in_proj(hs, ln1, w_pad, qg, kg, bs * ts)
        rest_pad = jnp.pad(rest.reshape(bs, ts, R_WIDTH), ((0, 0), (0, ts_pad - ts), (0, 0)))
        og, gla_st = _gla_group(rest_pad, _block_diag_expand(state_gla[l], True), wa2_pad, ba, gla_ng, gsum_b,
                                sample_group, ts_pad, ts)
        cb0 = jnp.pad(state_gdn_conv[l], ((0, 0), (8 - (CONV_K - 1), 0), (0, 0)))
        od, gdn_st = _gdn_group(rest_pad, cb0, _block_diag_expand(state_gdn[l], False), cw, alog_row, dtb_row,
                                gdn_ng, gsum_b, sample_group, ts_pad, ts)
        og = og[:, :ts].reshape(bs * ts, GLA_V)
        od = od[:, :ts].reshape(bs * ts, GDN_V)
        bias_rows = jnp.broadcast_to(jnp.tile(bias, ts)[:, None], (ts * SB_HEADS, 128)).astype(F32)
        osb = _sb_sample(page_table, qn.reshape(bs, ts * SB_HEADS, SB_DIM), bias_rows,
                         sk.reshape(bs, ts * SB_HEADS, SB_DIM), sv.reshape(bs, ts * SB_HEADS, SB_DIM), mu,
                         cache_sb_k, cache_sb_v, l, pages_per_step)
        osb = osb.reshape(bs * ts, SB_W).astype(BF16)
        hs = _out_ffn(hs, og, od, osb, wo, ln2, wu, wd, bs * ts)
        xp = jnp.concatenate([state_gdn_conv[l], rest.reshape(bs, ts, R_WIDTH)[:, :, R_DQKV:R_DQKV + GDN_CONV_DIM]],
                             axis=1)
        conv_s = xp[:, xp.shape[1] - (CONV_K - 1):]
        outs_s.append((sk.reshape(bs, ts, SB_HEADS, SB_DIM), sv.reshape(bs, ts, SB_HEADS, SB_DIM),
                       _block_diag_extract(gla_st, GLA_HEADS, True),
                       _block_diag_extract(gdn_st, GDN_HEADS, False), conv_s))

    sbk_p, sbv_p, gla_p, gdn_p, conv_p = [jnp.stack(a) for a in zip(*outs_p)]
    sbk_s, sbv_s, gla_s, gdn_s, conv_s = [jnp.stack(a) for a in zip(*outs_s)]
    return (hp.reshape(bp, tp, d), hs.reshape(bs, ts, d), sbk_p, sbv_p, gla_p, gdn_p, conv_p,
            sbk_s, sbv_s, gla_s, gdn_s, conv_s)
```

```python
import functools
import math

import jax
import jax.numpy as jnp
from jax import lax
from jax.experimental import pallas as pl
from jax.experimental.pallas import tpu as pltpu

F32 = jnp.float32
BF16 = jnp.bfloat16
HI = lax.Precision.HIGHEST

EPS = 1e-6
HEAD_DIM = 64
GLA_HEADS = 4
GLA_DK = 32
GLA_DV = 64
GLA_RANK = 16
GLA_TAU = 16.0
GDN_HEADS = 4
GDN_DK = 64
GDN_DV = 64
CONV_K = 4
SB_HEADS = 8
SB_DIM = 64
PAGE_SIZE = 128
CHUNK = 64
SB_TQ = 2048
SB_TK = 512
SB_PAGES_PER_STEP = 16
SAMPLE_GROUP = 16

GLA_QK = GLA_HEADS * GLA_DK
GLA_V = GLA_HEADS * GLA_DV
GDN_QK = GDN_HEADS * GDN_DK
GDN_V = GDN_HEADS * GDN_DV
GDN_CONV_DIM = 2 * GDN_QK + GDN_V
SB_W = SB_HEADS * SB_DIM

R_DQKV = 0
R_GV = 768
R_GZ = 1024
R_DZ = 1280
R_GQ = 1536
R_GK = 1664
R_SMALL = 1792
R_WIDTH = 1920
LANE_LR = 0
LANE_DB = 16
LANE_DA = 20
N_PAD = 3 * SB_W + R_WIDTH

VMEM_LIMIT = 56 * 1024 * 1024
LOG2E = 1.4426950408889634
SB_Q_SCALE = LOG2E * SB_DIM ** -0.5


def _dot(a, b, precision=None):
    return jnp.dot(a, b, precision=precision, preferred_element_type=F32)


def _dot_nt(a, b, precision=None):
    return lax.dot_general(a, b, (((1,), (1,)), ((), ())), precision=precision, preferred_element_type=F32)


def _dot_tn(a, b, precision=None):
    return lax.dot_general(a, b, (((0,), (0,)), ((), ())), precision=precision, preferred_element_type=F32)


def _softplus(x):
    return jnp.maximum(x, 0.0) + jnp.log1p(jnp.exp(-jnp.abs(x)))


def _sigmoid(x):
    return 1.0 / (1.0 + jnp.exp(-x))


def _silu(x):
    return x * _sigmoid(x)


def _headnorm(x, g):
    outs = []
    lane = lax.broadcasted_iota(jnp.int32, (1, 128), 1)
    first = lane < HEAD_DIM
    for j in range(x.shape[1] // 128):
        xb = x[:, j * 128:(j + 1) * 128]
        sq = xb * xb
        s0 = jnp.sum(jnp.where(first, sq, 0.0), axis=-1, keepdims=True)
        s1 = jnp.sum(jnp.where(first, 0.0, sq), axis=-1, keepdims=True)
        ms = jnp.where(first, s0, s1) * (1.0 / HEAD_DIM)
        outs.append(xb * lax.rsqrt(ms + EPS))
    return jnp.concatenate(outs, axis=-1) * g


def _in_proj_kernel(x_ref, g_ref, w_ref, qg_ref, kg_ref, q_ref, k_ref, v_ref, r_ref):
    x = x_ref[...]
    ms = jnp.mean(x * x, axis=-1, keepdims=True)
    xn = (x * lax.rsqrt(ms + EPS) * g_ref[...]).astype(BF16)
    sq = _dot(xn, w_ref[:, 0:SB_W])
    q_ref[...] = (_headnorm(sq, qg_ref[...]) * SB_Q_SCALE).astype(BF16)
    sk = _dot(xn, w_ref[:, SB_W:2 * SB_W])
    k_ref[...] = _headnorm(sk, kg_ref[...])
    v_ref[...] = _dot(xn, w_ref[:, 2 * SB_W:3 * SB_W])
    for c0 in range(0, R_WIDTH, 640):
        r_ref[:, c0:c0 + 640] = _dot(xn, w_ref[:, 3 * SB_W + c0:3 * SB_W + c0 + 640])


def _in_proj(h, ln_g, w_pad, qn_g, kn_g, tm):
    m = h.shape[0]
    row = lambda i: (i, 0)
    const = lambda i: (0, 0)
    return pl.pallas_call(
        _in_proj_kernel,
        grid=(m // tm,),
        in_specs=[
            pl.BlockSpec((tm, h.shape[1]), row),
            pl.BlockSpec((1, h.shape[1]), const),
            pl.BlockSpec(w_pad.shape, const, pipeline_mode=pl.Buffered(1)),
            pl.BlockSpec((1, SB_W), const),
            pl.BlockSpec((1, SB_W), const),
        ],
        out_specs=[
            pl.BlockSpec((tm, SB_W), row),
            pl.BlockSpec((tm, SB_W), row),
            pl.BlockSpec((tm, SB_W), row),
            pl.BlockSpec((tm, R_WIDTH), row),
        ],
        out_shape=[
            jax.ShapeDtypeStruct((m, SB_W), BF16),
            jax.ShapeDtypeStruct((m, SB_W), F32),
            jax.ShapeDtypeStruct((m, SB_W), F32),
            jax.ShapeDtypeStruct((m, R_WIDTH), F32),
        ],
        compiler_params=pltpu.CompilerParams(dimension_semantics=("arbitrary",), vmem_limit_bytes=VMEM_LIMIT),
        name="in_proj",
    )(h, ln_g, w_pad, qn_g, kn_g)


def _out_ffn_kernel(h_ref, og_ref, od_ref, os_ref, wo_ref, g_ref, wu_ref, wd_ref, o_ref, *, f_chunk):
    mixed = jnp.concatenate([og_ref[...], od_ref[...], os_ref[...]], axis=-1)
    h = h_ref[...] + _dot(mixed, wo_ref[...])
    ms = jnp.mean(h * h, axis=-1, keepdims=True)
    hn = (h * lax.rsqrt(ms + EPS) * g_ref[...]).astype(BF16)
    d_ff = wu_ref.shape[1]
    acc = h
    for f0 in range(0, d_ff, f_chunk):
        u = _dot(hn, wu_ref[:, f0:f0 + f_chunk])
        u = jnp.maximum(u, 0.0)
        acc = acc + _dot((u * u).astype(BF16), wd_ref[f0:f0 + f_chunk, :])
    o_ref[...] = acc


def _out_ffn(h, og, od, osb, wo, ln_g, wu, wd, tm):
    m, d = h.shape
    row = lambda i: (i, 0)
    const = lambda i: (0, 0)
    return pl.pallas_call(
        functools.partial(_out_ffn_kernel, f_chunk=1024),
        grid=(m // tm,),
        in_specs=[
            pl.BlockSpec((tm, d), row),
            pl.BlockSpec((tm, GLA_V), row),
            pl.BlockSpec((tm, GDN_V), row),
            pl.BlockSpec((tm, SB_W), row),
            pl.BlockSpec(wo.shape, const, pipeline_mode=pl.Buffered(1)),
            pl.BlockSpec((1, d), const),
            pl.BlockSpec(wu.shape, const, pipeline_mode=pl.Buffered(1)),
            pl.BlockSpec(wd.shape, const, pipeline_mode=pl.Buffered(1)),
        ],
        out_specs=pl.BlockSpec((tm, d), row),
        out_shape=jax.ShapeDtypeStruct((m, d), F32),
        compiler_params=pltpu.CompilerParams(dimension_semantics=("arbitrary",), vmem_limit_bytes=VMEM_LIMIT),
        name="out_ffn",
    )(h, og, od, osb, wo, ln_g, wu, wd)


def _row_col(c):
    row = lax.broadcasted_iota(jnp.int32, (c, c), 0)
    col = lax.broadcasted_iota(jnp.int32, (c, c), 1)
    return row, col


def _gla_kernel(q_ref, k_ref, v_ref, z_ref, sm_ref, s0_ref, wa2_ref, ba_ref, ng_ref, gmean_ref,
                o_ref, sout_ref, st_scr, *, chunk, t_valid):
    c = chunk

    @pl.when(pl.program_id(1) == 0)
    def _():
        st_scr[...] = s0_ref[...]

    q = q_ref[...] * (GLA_DK ** -0.5)
    k = k_ref[...]
    v = v_ref[...]
    x = _dot(sm_ref[...], wa2_ref[...], HI) + ba_ref[...]
    log_a = -_softplus(-x) * (1.0 / GLA_TAU)
    if t_valid < c:
        valid = lax.broadcasted_iota(jnp.int32, (c, 1), 0) < t_valid
        log_a = jnp.where(valid, log_a, 0.0)
        q = jnp.where(valid, q, 0.0)
        k = jnp.where(valid, k, 0.0)
        v = jnp.where(valid, v, 0.0)
    row, col = _row_col(c)
    incl = row >= col
    b = _dot(incl.astype(F32), log_a, HI)
    b_last = b[c - 1:c, :]
    b_mid = b[c // 2 - 1:c // 2, :]
    q_in = (q * jnp.exp(b)).astype(BF16)
    q_t = q * jnp.exp(b - b_mid)
    k_t = (k * jnp.exp(b_mid - b)).astype(BF16)
    k_st = (k * jnp.exp(b_last - b)).astype(BF16)
    vb = v.astype(BF16)

    st = st_scr[...]
    o = _dot_nt(q_in, st.astype(BF16))
    lane_k = lax.broadcasted_iota(jnp.int32, (1, GLA_QK), 1) // GLA_DK
    lane_v = lax.broadcasted_iota(jnp.int32, (1, GLA_V), 1) // GLA_DV
    for h in range(GLA_HEADS):
        qh = jnp.where(lane_k == h, q_t, 0.0).astype(BF16)
        att = jnp.where(incl, _dot_nt(qh, k_t), 0.0).astype(BF16)
        o = o + _dot(att, jnp.where(lane_v == h, vb, jnp.zeros_like(vb)))

    bd = (lax.broadcasted_iota(jnp.int32, (GLA_V, GLA_QK), 0) // GLA_DV
          == lax.broadcasted_iota(jnp.int32, (GLA_V, GLA_QK), 1) // GLA_DK)
    st_new = st * jnp.exp(b_last) + jnp.where(bd, _dot_tn(vb, k_st), 0.0)
    st_scr[...] = st_new
    sout_ref[...] = st_new

    ms = _dot(o * o, gmean_ref[...], HI)
    y = o * lax.rsqrt(ms + EPS) * ng_ref[...]
    o_ref[...] = (y * _silu(z_ref[...])).astype(BF16)


def _gla(rest, s0t, wa2_pad, ba, ng_t, gmean, n_seq, seq_len, chunk, t_valid):
    nc = seq_len // chunk
    blk = lambda width, cb: pl.BlockSpec((chunk, width), lambda b, c: (b * nc + c, cb))
    const2 = lambda b, c: (0, 0)
    return pl.pallas_call(
        functools.partial(_gla_kernel, chunk=chunk, t_valid=t_valid),
        grid=(n_seq, nc),
        in_specs=[
            blk(GLA_QK, R_GQ // GLA_QK),
            blk(GLA_QK, R_GK // GLA_QK),
            blk(GLA_V, R_GV // GLA_V),
            blk(GLA_V, R_GZ // GLA_V),
            blk(128, R_SMALL // 128),
            pl.BlockSpec((None, GLA_V, GLA_QK), lambda b, c: (b, 0, 0)),
            pl.BlockSpec((128, GLA_QK), const2),
            pl.BlockSpec((1, GLA_QK), const2),
            pl.BlockSpec((1, GLA_V), const2),
            pl.BlockSpec((GLA_V, GLA_V), const2),
        ],
        out_specs=[
            pl.BlockSpec((chunk, GLA_V), lambda b, c: (b * nc + c, 0)),
            pl.BlockSpec((None, GLA_V, GLA_QK), lambda b, c: (b, 0, 0)),
        ],
        out_shape=[
            jax.ShapeDtypeStruct((n_seq * seq_len, GLA_V), BF16),
            jax.ShapeDtypeStruct((n_seq, GLA_V, GLA_QK), F32),
        ],
        scratch_shapes=[pltpu.VMEM((GLA_V, GLA_QK), F32)],
        compiler_params=pltpu.CompilerParams(dimension_semantics=("arbitrary", "arbitrary")),
        name="gla",
    )(rest, rest, rest, rest, rest, s0t, wa2_pad, ba, ng_t, gmean)


def _gdn_kernel(x_ref, sm_ref, z_ref, cb0_ref, s0_ref, cw_ref, alog_ref, dtb_ref, ng_ref, gsum_ref,
                expb_ref, expg_ref, o_ref, sout_ref, xp_scr, s_scr, *, chunk, t_valid):
    c = chunk

    @pl.when(pl.program_id(1) == 0)
    def _():
        xp_scr[0:8, :] = cb0_ref[...]
        s_scr[...] = s0_ref[...]

    xp_scr[8:8 + c, :] = x_ref[...]
    base = 8 - (CONV_K - 1)
    conv = xp_scr[base:base + c, :] * cw_ref[0:1, :]
    for i in range(1, CONV_K):
        conv = conv + xp_scr[base + i:base + i + c, :] * cw_ref[i:i + 1, :]
    xp_scr[0:8, :] = xp_scr[c:c + 8, :]
    conv = _silu(conv)
    cq = conv[:, 0:GDN_QK]
    ck = conv[:, GDN_QK:2 * GDN_QK]
    v = conv[:, 2 * GDN_QK:]
    gsum = gsum_ref[...]
    q = cq * lax.rsqrt(_dot(cq * cq, gsum, HI) + EPS) * (GDN_DK ** -0.5)
    k = ck * lax.rsqrt(_dot(ck * ck, gsum, HI) + EPS)

    sm = sm_ref[...]
    beta_all = _sigmoid(sm)
    g_all = -jnp.exp(alog_ref[...]) * _softplus(sm + dtb_ref[...])
    if t_valid < c:
        valid = lax.broadcasted_iota(jnp.int32, (c, 1), 0) < t_valid
        q = jnp.where(valid, q, 0.0)
        k = jnp.where(valid, k, 0.0)
        v = jnp.where(valid, v, 0.0)
        beta_all = jnp.where(valid, beta_all, 0.0)
        g_all = jnp.where(valid, g_all, 0.0)
    row, col = _row_col(c)
    incl = row >= col
    strict = row > col
    b_col = _dot(incl.astype(F32), g_all, HI)
    bfull = _dot(b_col, expg_ref[...], HI)
    beta_full = _dot(beta_all, expb_ref[...], HI)
    b_last = bfull[c - 1:c, :]
    eb = jnp.exp(bfull)
    kb = k * beta_full
    rv = v * beta_full
    rk = kb * eb
    q_in = (q * eb).astype(BF16)
    k_st = (k * jnp.exp(b_last - bfull)).astype(BF16)
    kbf = k.astype(BF16)

    lane_h = lax.broadcasted_iota(jnp.int32, (1, GDN_V), 1) // GDN_DV
    lane_s = lax.broadcasted_iota(jnp.int32, (c, 128), 1)
    eye = (row == col).astype(F32)
    n_sq = int(round(math.log2(c))) - 1
    u = jnp.zeros((c, GDN_V), F32)
    w = jnp.zeros((c, GDN_V), F32)
    atts = []
    for h in range(GDN_HEADS):
        mh = lane_h == h
        sel = (lane_s == LANE_DA + h).astype(F32)
        b_s = _dot_nt(sel, b_col, HI)
        b_t = jnp.broadcast_to(b_col[:, LANE_DA + h:LANE_DA + h + 1], (c, c))
        dec = jnp.exp(jnp.minimum(b_t - b_s, 0.0))
        a = jnp.where(strict, _dot_nt(jnp.where(mh, kb, 0.0), k, HI) * dec, 0.0)
        p = eye - a
        qq = a
        for _ in range(n_sq):
            qq = _dot(qq, qq, HI)
            p = p + _dot(p, qq, HI)
        u = u + _dot(p, jnp.where(mh, rv, 0.0), HI)
        w = w + _dot(p, jnp.where(mh, rk, 0.0), HI)
        att = _dot_nt(jnp.where(mh, q, 0.0).astype(BF16), kbf)
        atts.append(jnp.where(incl, att * dec, 0.0).astype(BF16))

    s = s_scr[...]
    sb = s.astype(BF16)
    v_new = u - _dot(w.astype(BF16), sb)
    vnb = v_new.astype(BF16)
    o = _dot(q_in, sb)
    for h in range(GDN_HEADS):
        o = o + _dot(atts[h], jnp.where(lane_h == h, vnb, jnp.zeros_like(vnb)))
    bd = (lax.broadcasted_iota(jnp.int32, (GDN_QK, GDN_V), 0) // GDN_DK
          == lax.broadcasted_iota(jnp.int32, (GDN_QK, GDN_V), 1) // GDN_DV)
    s_new = s * jnp.exp(b_last) + jnp.where(bd, _dot_tn(k_st, vnb), 0.0)
    s_scr[...] = s_new
    sout_ref[...] = s_new

    gmean = gsum * (1.0 / GDN_DV)
    ms = _dot(o * o, gmean, HI)
    y = o * lax.rsqrt(ms + EPS) * ng_ref[...]
    o_ref[...] = (y * _silu(z_ref[...])).astype(BF16)


def _gdn(rest, cb0, s0, cw, alog_row, dtb_row, ng_t, gsum, expb, expg, n_seq, seq_len, chunk, t_valid):
    nc = seq_len // chunk
    blk = lambda width, cb: pl.BlockSpec((chunk, width), lambda b, c: (b * nc + c, cb))
    const2 = lambda b, c: (0, 0)
    return pl.pallas_call(
        functools.partial(_gdn_kernel, chunk=chunk, t_valid=t_valid),
        grid=(n_seq, nc),
        in_specs=[
            blk(GDN_CONV_DIM, R_DQKV // GDN_CONV_DIM),
            blk(128, R_SMALL // 128),
            blk(GDN_V, R_DZ // GDN_V),
            pl.BlockSpec((None, 8, GDN_CONV_DIM), lambda b, c: (b, 0, 0)),
            pl.BlockSpec((None, GDN_QK, GDN_V), lambda b, c: (b, 0, 0)),
            pl.BlockSpec((CONV_K, GDN_CONV_DIM), const2),
            pl.BlockSpec((1, 128), const2),
            pl.BlockSpec((1, 128), const2),
            pl.BlockSpec((1, GDN_V), const2),
            pl.BlockSpec((GDN_V, GDN_V), const2),
            pl.BlockSpec((128, GDN_V), const2),
            pl.BlockSpec((128, GDN_V), const2),
        ],
        out_specs=[
            pl.BlockSpec((chunk, GDN_V), lambda b, c: (b * nc + c, 0)),
            pl.BlockSpec((None, GDN_QK, GDN_V), lambda b, c: (b, 0, 0)),
        ],
        out_shape=[
            jax.ShapeDtypeStruct((n_seq * seq_len, GDN_V), BF16),
            jax.ShapeDtypeStruct((n_seq, GDN_QK, GDN_V), F32),
        ],
        scratch_shapes=[pltpu.VMEM((chunk + 8, GDN_CONV_DIM), F32), pltpu.VMEM((GDN_QK, GDN_V), F32)],
        compiler_params=pltpu.CompilerParams(dimension_semantics=("arbitrary", "arbitrary")),
        name="gdn",
    )(rest, rest, rest, cb0, s0, cw, alog_row, dtb_row, ng_t, gsum, expb, expg)


def _split(x):
    hi = x.astype(BF16)
    return hi, (x - hi.astype(F32)).astype(BF16)


def _dot_x3(a, b):
    ah, al = _split(a)
    bh, bl = _split(b)
    return _dot(ah, bh) + (_dot(al, bh) + _dot(ah, bl))


def _dot_x2(a, b_exact):
    ah, al = _split(a)
    return _dot(ah, b_exact) + _dot(al, b_exact)


def _per_head_lanes(cols, first_lane, heads, width):
    c = cols.shape[0]
    lane_h = lax.broadcasted_iota(jnp.int32, (1, heads * width), 1) // width
    out = jnp.broadcast_to(cols[:, first_lane:first_lane + 1], (c, heads * width))
    for h in range(1, heads):
        out = jnp.where(lane_h == h, jnp.broadcast_to(cols[:, first_lane + h:first_lane + h + 1], (c, heads * width)), out)
    return out


def _gla_chunk(q, k, v, z, sm, st, wa2, ba, ng, gmean_b, *, c, t_valid):
    q = q * (GLA_DK ** -0.5)
    x = _dot_x3(sm, wa2) + ba
    log_a = -_softplus(-x) * (1.0 / GLA_TAU)
    if t_valid < c:
        valid = lax.broadcasted_iota(jnp.int32, (c, 1), 0) < t_valid
        log_a = jnp.where(valid, log_a, 0.0)
        q = jnp.where(valid, q, 0.0)
        k = jnp.where(valid, k, 0.0)
        v = jnp.where(valid, v, 0.0)
    row, col = _row_col(c)
    incl = row >= col
    b = _dot_x2_lhs_exact(incl.astype(BF16), log_a)
    b_last = b[c - 1:c, :]
    b_mid = b[c // 2 - 1:c // 2, :]
    q_in = (q * jnp.exp(b)).astype(BF16)
    q_t = q * jnp.exp(b - b_mid)
    k_t = (k * jnp.exp(b_mid - b)).astype(BF16)
    k_st = (k * jnp.exp(b_last - b)).astype(BF16)
    vb = v.astype(BF16)

    o = _dot_nt(q_in, st.astype(BF16))
    lane_k = lax.broadcasted_iota(jnp.int32, (1, GLA_QK), 1) // GLA_DK
    lane_v = lax.broadcasted_iota(jnp.int32, (1, GLA_V), 1) // GLA_DV
    for h in range(GLA_HEADS):
        qh = jnp.where(lane_k == h, q_t, 0.0).astype(BF16)
        att = jnp.where(incl, _dot_nt(qh, k_t), 0.0).astype(BF16)
        o = o + _dot(att, jnp.where(lane_v == h, vb, jnp.zeros_like(vb)))

    bd = (lax.broadcasted_iota(jnp.int32, (GLA_V, GLA_QK), 0) // GLA_DV
          == lax.broadcasted_iota(jnp.int32, (GLA_V, GLA_QK), 1) // GLA_DK)
    st_new = st * jnp.exp(b_last) + jnp.where(bd, _dot_tn(vb, k_st), 0.0)
    ms = _dot_x2(o * o, gmean_b) * (1.0 / GLA_DV)
    y = o * lax.rsqrt(ms + EPS) * ng
    return (y * _silu(z)).astype(BF16), st_new


def _dot_x2_lhs_exact(a_exact, b):
    bh, bl = _split(b)
    return _dot(a_exact, bh) + _dot(a_exact, bl)


def _gla_group_kernel(q_ref, k_ref, v_ref, z_ref, sm_ref, s0_ref, wa2_ref, ba_ref, ng_ref, gsum_ref,
                      o_ref, sout_ref, st_scr, *, group, chunk, t_valid):
    @pl.when(pl.program_id(1) == 0)
    def _():
        st_scr[...] = s0_ref[...]

    for g in range(group):
        y, st_new = _gla_chunk(q_ref[g], k_ref[g], v_ref[g], z_ref[g], sm_ref[g], st_scr[g], wa2_ref[...],
                               ba_ref[...], ng_ref[...], gsum_ref[...], c=chunk, t_valid=t_valid)
        o_ref[g] = y
        st_scr[g] = st_new
        sout_ref[g] = st_new


def _gla_group(rest3, s0t, wa2_pad, ba, ng_t, gsum_b, group, chunk, t_valid):
    n_seq, seq_len, _ = rest3.shape
    nc = seq_len // chunk
    blk = lambda width, cb: pl.BlockSpec((group, chunk, width), lambda b, c: (b, c, cb))
    const2 = lambda b, c: (0, 0)
    return pl.pallas_call(
        functools.partial(_gla_group_kernel, group=group, chunk=chunk, t_valid=t_valid),
        grid=(n_seq // group, nc),
        in_specs=[
            blk(GLA_QK, R_GQ // GLA_QK),
            blk(GLA_QK, R_GK // GLA_QK),
            blk(GLA_V, R_GV // GLA_V),
            blk(GLA_V, R_GZ // GLA_V),
            blk(128, R_SMALL // 128),
            pl.BlockSpec((group, GLA_V, GLA_QK), lambda b, c: (b, 0, 0)),
            pl.BlockSpec((128, GLA_QK), const2),
            pl.BlockSpec((1, GLA_QK), const2),
            pl.BlockSpec((1, GLA_V), const2),
            pl.BlockSpec((GLA_V, GLA_V), const2),
        ],
        out_specs=[
            pl.BlockSpec((group, chunk, GLA_V), lambda b, c: (b, c, 0)),
            pl.BlockSpec((group, GLA_V, GLA_QK), lambda b, c: (b, 0, 0)),
        ],
        out_shape=[
            jax.ShapeDtypeStruct((n_seq, seq_len, GLA_V), BF16),
            jax.ShapeDtypeStruct((n_seq, GLA_V, GLA_QK), F32),
        ],
        scratch_shapes=[pltpu.VMEM((group, GLA_V, GLA_QK), F32)],
        compiler_params=pltpu.CompilerParams(dimension_semantics=("arbitrary", "arbitrary"),
                                             vmem_limit_bytes=VMEM_LIMIT),
        name="gla",
    )(rest3, rest3, rest3, rest3, rest3, s0t, wa2_pad, ba, ng_t, gsum_b)


def _gdn_pre(xp_ref, x, sm, cw, alog, dtb, gsum_b, *, c, t_valid):
    xp_ref[8:8 + c, :] = x
    base = 8 - (CONV_K - 1)
    conv = xp_ref[base:base + c, :] * cw[0:1, :]
    for i in range(1, CONV_K):
        conv = conv + xp_ref[base + i:base + i + c, :] * cw[i:i + 1, :]
    xp_ref[0:8, :] = xp_ref[c:c + 8, :]
    conv = _silu(conv)
    cq = conv[:, 0:GDN_QK]
    ck = conv[:, GDN_QK:2 * GDN_QK]
    v = conv[:, 2 * GDN_QK:]
    q = cq * lax.rsqrt(_dot_x2(cq * cq, gsum_b) + EPS) * (GDN_DK ** -0.5)
    k = ck * lax.rsqrt(_dot_x2(ck * ck, gsum_b) + EPS)

    beta_all = _sigmoid(sm)
    g_all = -jnp.exp(alog) * _softplus(sm + dtb)
    if t_valid < c:
        valid = lax.broadcasted_iota(jnp.int32, (c, 1), 0) < t_valid
        q = jnp.where(valid, q, 0.0)
        k = jnp.where(valid, k, 0.0)
        v = jnp.where(valid, v, 0.0)
        beta_all = jnp.where(valid, beta_all, 0.0)
        g_all = jnp.where(valid, g_all, 0.0)
    row, col = _row_col(c)
    b_col = _dot_x2_lhs_exact((row >= col).astype(BF16), g_all)
    bfull = _per_head_lanes(b_col, LANE_DA, GDN_HEADS, GDN_DV)
    beta_full = _per_head_lanes(beta_all, LANE_DB, GDN_HEADS, GDN_DV)
    b_last = bfull[c - 1:c, :]
    eb = jnp.exp(bfull)
    kb = k * beta_full
    return dict(
        b_col=b_col, b_row=b_col.T, b_last=b_last,
        q_in=(q * eb).astype(BF16), k_st=(k * jnp.exp(b_last - bfull)).astype(BF16),
        kbf=k.astype(BF16), kbb=kb.astype(BF16), qb=q.astype(BF16),
        rhs=jnp.concatenate([v * beta_full, kb * eb], axis=1).astype(BF16))


def _gdn_post(pre, uw, atts, s, z, ng, gsum_b):
    lane_h = lax.broadcasted_iota(jnp.int32, (1, GDN_V), 1) // GDN_DV
    sb = s.astype(BF16)
    v_new = uw[:, :GDN_V] - _dot(uw[:, GDN_V:].astype(BF16), sb)
    vnb = v_new.astype(BF16)
    o = _dot(pre["q_in"], sb)
    for h in range(GDN_HEADS):
        o = o + _dot(atts[h], jnp.where(lane_h == h, vnb, jnp.zeros_like(vnb)))
    bd = (lax.broadcasted_iota(jnp.int32, (GDN_QK, GDN_V), 0) // GDN_DK
          == lax.broadcasted_iota(jnp.int32, (GDN_QK, GDN_V), 1) // GDN_DV)
    s_new = s * jnp.exp(pre["b_last"]) + jnp.where(bd, _dot_tn(pre["k_st"], vnb), 0.0)
    ms = _dot_x2(o * o, gsum_b) * (1.0 / GDN_DV)
    y = o * lax.rsqrt(ms + EPS) * ng
    return (y * _silu(z)).astype(BF16), s_new


def _gdn_group_kernel(x_ref, sm_ref, z_ref, cb0_ref, s0_ref, cw_ref, alog_ref, dtb_ref, ng_ref, gsum_ref,
                      o_ref, sout_ref, xp_scr, s_scr, *, group, chunk, t_valid):
    c = chunk

    @pl.when(pl.program_id(1) == 0)
    def _():
        xp_scr[:, 0:8, :] = cb0_ref[...]
        s_scr[...] = s0_ref[...]

    gsum_b = gsum_ref[...]
    pre = [_gdn_pre(xp_scr.at[g], x_ref[g], sm_ref[g], cw_ref[...], alog_ref[...], dtb_ref[...], gsum_b,
                    c=c, t_valid=t_valid) for g in range(group)]

    row, col = _row_col(c)
    incl = row >= col
    strict = row > col
    eye = (row == col).astype(F32)
    lane_h = lax.broadcasted_iota(jnp.int32, (1, GDN_V), 1) // GDN_DV
    lane_h2 = lax.broadcasted_iota(jnp.int32, (1, 2 * GDN_V), 1) % GDN_V // GDN_DV
    chains = [(g, h) for g in range(group) for h in range(GDN_HEADS)]
    dec, qq, p, atts = {}, {}, {}, {}
    for g, h in chains:
        pg = pre[g]
        b_t = jnp.broadcast_to(pg["b_col"][:, LANE_DA + h:LANE_DA + h + 1], (c, c))
        b_s = jnp.broadcast_to(pg["b_row"][LANE_DA + h:LANE_DA + h + 1, :], (c, c))
        dec[g, h] = jnp.exp(jnp.minimum(b_t - b_s, 0.0))
    for g, h in chains:
        pg = pre[g]
        kbh = jnp.where(lane_h == h, pg["kbb"], jnp.zeros_like(pg["kbb"]))
        a = jnp.where(strict, _dot_nt(kbh, pg["kbf"]) * dec[g, h], 0.0)
        qq[g, h] = a
        p[g, h] = eye - a
    for _ in range(int(round(math.log2(c))) - 1):
        for ch in chains:
            qq[ch] = _dot_x3(qq[ch], qq[ch])
        for ch in chains:
            p[ch] = p[ch] + _dot_x3(p[ch], qq[ch])
    for g, h in chains:
        pg = pre[g]
        qh = jnp.where(lane_h == h, pg["qb"], jnp.zeros_like(pg["qb"]))
        atts[g, h] = jnp.where(incl, _dot_nt(qh, pg["kbf"]) * dec[g, h], 0.0).astype(BF16)
    uws = []
    for g in range(group):
        rhs = pre[g]["rhs"]
        uw = _dot(p[g, 0].astype(BF16), jnp.where(lane_h2 == 0, rhs, jnp.zeros_like(rhs)))
        for h in range(1, GDN_HEADS):
            uw = uw + _dot(p[g, h].astype(BF16), jnp.where(lane_h2 == h, rhs, jnp.zeros_like(rhs)))
        uws.append(uw)

    for g in range(group):
        y, s_new = _gdn_post(pre[g], uws[g], [atts[g, h] for h in range(GDN_HEADS)], s_scr[g], z_ref[g],
                             ng_ref[...], gsum_b)
        o_ref[g] = y
        s_scr[g] = s_new
        sout_ref[g] = s_new


def _gdn_group(rest3, cb0, s0, cw, alog_row, dtb_row, ng_t, gsum_b, group, chunk, t_valid):
    n_seq, seq_len, _ = rest3.shape
    nc = seq_len // chunk
    blk = lambda width, cb: pl.BlockSpec((group, chunk, width), lambda b, c: (b, c, cb))
    const2 = lambda b, c: (0, 0)
    return pl.pallas_call(
        functools.partial(_gdn_group_kernel, group=group, chunk=chunk, t_valid=t_valid),
        grid=(n_seq // group, nc),
        in_specs=[
            blk(GDN_CONV_DIM, R_DQKV // GDN_CONV_DIM),
            blk(128, R_SMALL // 128),
            blk(GDN_V, R_DZ // GDN_V),
            pl.BlockSpec((group, 8, GDN_CONV_DIM), lambda b, c: (b, 0, 0)),
            pl.BlockSpec((group, GDN_QK, GDN_V), lambda b, c: (b, 0, 0)),
            pl.BlockSpec((CONV_K, GDN_CONV_DIM), const2),
            pl.BlockSpec((1, 128), const2),
            pl.BlockSpec((1, 128), const2),
            pl.BlockSpec((1, GDN_V), const2),
            pl.BlockSpec((GDN_V, GDN_V), const2),
        ],
        out_specs=[
            pl.BlockSpec((group, chunk, GDN_V), lambda b, c: (b, c, 0)),
            pl.BlockSpec((group, GDN_QK, GDN_V), lambda b, c: (b, 0, 0)),
        ],
        out_shape=[
            jax.ShapeDtypeStruct((n_seq, seq_len, GDN_V), BF16),
            jax.ShapeDtypeStruct((n_seq, GDN_QK, GDN_V), F32),
        ],
        scratch_shapes=[pltpu.VMEM((group, chunk + 8, GDN_CONV_DIM), F32), pltpu.VMEM((group, GDN_QK, GDN_V), F32)],
        compiler_params=pltpu.CompilerParams(dimension_semantics=("arbitrary", "arbitrary"),
                                             vmem_limit_bytes=VMEM_LIMIT),
        name="gdn",
    )(rest3, rest3, rest3, cb0, s0, cw, alog_row, dtb_row, ng_t, gsum_b)


def _sb_tile(z, c, mask, mu):
    tk = z.shape[1]
    l = jnp.log2(1.0 + jnp.exp2(-jnp.abs(z)))
    log_beta = jnp.minimum(z, 0.0) - l
    log_1m = log_beta - z
    if mask is not None:
        log_1m = jnp.where(mask, log_1m, 0.0)
    l16 = log_1m.astype(BF16)
    xs = []
    for s in range(tk // 128 - 1, -1, -1):
        sc = _dot(l16[:, s * 128:(s + 1) * 128], mu)
        xs.append(log_beta[:, s * 128:(s + 1) * 128] + sc[:, :128] + c)
        c = c + sc[:, 128:]
    x = xs[0] if len(xs) == 1 else jnp.concatenate(xs[::-1], axis=1)
    att = jnp.exp2(x)
    if mask is not None:
        att = jnp.where(mask, att, 0.0)
    return att.astype(BF16), c


def _sb_prompt_kernel(bias_ref, q_ref, k_ref, v_ref, mu_ref, o_ref, kb_scr, vb_scr, c_scr, o_scr, *, tq, tk):
    hp = pl.program_id(1)
    i = pl.program_id(2)

    @pl.when(i == 0)
    def _():
        kb_scr[...] = k_ref[...].astype(BF16)
        vb_scr[...] = v_ref[...].astype(BF16)

    q = q_ref[...]
    lane = lax.broadcasted_iota(jnp.int32, (1, 2 * SB_DIM), 1)
    first = lane < SB_DIM
    zero = jnp.zeros_like(q)
    qs = (jnp.where(first, q, zero), jnp.where(first, zero, q))
    bias = (bias_ref[2 * hp], bias_ref[2 * hp + 1])
    mu = mu_ref[...]
    c_scr[...] = jnp.zeros_like(c_scr)
    o_scr[...] = jnp.zeros_like(o_scr)
    n_sub = tq // tk

    def tile(j, masked, r0):
        start = pl.multiple_of(j * tk, tk)
        kblk = kb_scr[pl.ds(start, tk), :]
        vblk = vb_scr[pl.ds(start, tk), :]
        mask = None
        if masked:
            q_pos = i * tq + r0 + lax.broadcasted_iota(jnp.int32, (tq - r0, tk), 0)
            mask = start + lax.broadcasted_iota(jnp.int32, (tq - r0, tk), 1) < q_pos
        zs = [_dot_nt(qs[hh][r0:], kblk) + bias[hh] for hh in range(2)]
        res = _sb_chains([([zs[hh]], c_scr[hh, r0:], [mask]) for hh in range(2)], mu)
        for hh in range(2):
            (att,), c = res[hh]
            c_scr[hh, r0:] = c
            o_scr[hh, r0:] = o_scr[hh, r0:] + _dot(att, vblk)

    for dd in range(n_sub - 1, -1, -1):
        tile(i * n_sub + dd, True, dd * tk)

    def body(jj, carry):
        tile(i * n_sub - 1 - jj, False, 0)
        return carry

    lax.fori_loop(0, i * n_sub, body, 0)
    o_ref[...] = jnp.where(first, o_scr[0], o_scr[1]).astype(BF16)


def _sb_prompt(qn, sk, sv, bias, mu, n_seq, seq_len, tq, tk):
    nq = seq_len // tq
    n_hp = SB_HEADS // 2
    grid_spec = pltpu.PrefetchScalarGridSpec(
        num_scalar_prefetch=1,
        grid=(n_seq, n_hp, nq),
        in_specs=[
            pl.BlockSpec((tq, 2 * SB_DIM), lambda b, hp, i, bias: (b * nq + i, hp)),
            pl.BlockSpec((seq_len, 2 * SB_DIM), lambda b, hp, i, bias: (b, hp)),
            pl.BlockSpec((seq_len, 2 * SB_DIM), lambda b, hp, i, bias: (b, hp)),
            pl.BlockSpec((128, 256), lambda b, hp, i, bias: (0, 0)),
        ],
        out_specs=pl.BlockSpec((tq, 2 * SB_DIM), lambda b, hp, i, bias: (b * nq + i, hp)),
        scratch_shapes=[pltpu.VMEM((seq_len, 2 * SB_DIM), BF16), pltpu.VMEM((seq_len, 2 * SB_DIM), BF16),
                        pltpu.VMEM((2, tq, 128), F32), pltpu.VMEM((2, tq, 2 * SB_DIM), F32)],
    )
    return pl.pallas_call(
        functools.partial(_sb_prompt_kernel, tq=tq, tk=tk),
        grid_spec=grid_spec,
        out_shape=jax.ShapeDtypeStruct((n_seq * seq_len, SB_W), BF16),
        compiler_params=pltpu.CompilerParams(dimension_semantics=("arbitrary", "arbitrary", "arbitrary"),
                                             vmem_limit_bytes=VMEM_LIMIT),
        name="sb_prompt",
    )(bias, qn, sk, sv, mu)


def _sb_sample_kernel(pt_ref, q_ref, brow_ref, kn_ref, vn_ref, mu_ref, *rest, n_op, t_new):
    k_refs = rest[:n_op]
    v_refs = rest[n_op:2 * n_op]
    o_ref = rest[2 * n_op]
    c_scr, acc_scr = rest[2 * n_op + 1:]
    del pt_ref
    step = pl.program_id(1)
    q = q_ref[...]
    rows = q.shape[0]
    page_rows = PAGE_SIZE * SB_HEADS
    mu = mu_ref[...]
    bias = jnp.concatenate([brow_ref[...]] * (page_rows // 128), axis=1)
    h_of_row = lax.broadcasted_iota(jnp.int32, (rows, page_rows), 0) % SB_HEADS
    lane = lax.broadcasted_iota(jnp.int32, (rows, page_rows), 1)
    valid = lane % SB_HEADS == h_of_row

    def page(kp, vp, mask, c, acc):
        att, c = _sb_tile(_dot_nt(q, kp) + bias[:, :kp.shape[0]], c, mask, mu)
        return c, acc + _dot(att, vp)

    @pl.when(step == 0)
    def _():
        n_new = kn_ref.shape[0]
        pad = jnp.zeros((128 - n_new, SB_DIM), BF16)
        kn = jnp.concatenate([kn_ref[...].astype(BF16), pad], axis=0)
        vn = jnp.concatenate([vn_ref[...].astype(BF16), pad], axis=0)
        row_n = lax.broadcasted_iota(jnp.int32, (rows, 128), 0)
        lane_n = lax.broadcasted_iota(jnp.int32, (rows, 128), 1)
        mask = (lane_n % SB_HEADS == row_n % SB_HEADS) & (lane_n // SB_HEADS < row_n // SB_HEADS)
        c, acc = page(kn, vn, mask, jnp.zeros((rows, 128), F32), jnp.zeros((rows, SB_DIM), F32))
        c_scr[...] = c
        acc_scr[...] = acc

    c = c_scr[...]
    acc = acc_scr[...]
    for p in range(n_op - 1, -1, -1):
        kp = k_refs[p][...].reshape(page_rows, SB_DIM).astype(BF16)
        vp = v_refs[p][...].reshape(page_rows, SB_DIM).astype(BF16)
        c, acc = page(kp, vp, valid, c, acc)
    c_scr[...] = c
    acc_scr[...] = acc
    o_ref[...] = acc


def _sb_sample(page_table, q, bias_rows, kn, vn, mu, cache_k, cache_v, layer, n_op):
    n_b, n_pages = page_table.shape
    rows = q.shape[1]
    t_new = rows // SB_HEADS
    n_steps = n_pages // n_op

    def page_spec(p):
        return pl.BlockSpec((None, None, PAGE_SIZE, SB_HEADS, SB_DIM),
                            lambda b, s, pt: (layer, pt[b, (n_steps - 1 - s) * n_op + p], 0, 0, 0))

    grid_spec = pltpu.PrefetchScalarGridSpec(
        num_scalar_prefetch=1,
        grid=(n_b, n_steps),
        in_specs=[
            pl.BlockSpec((None, rows, SB_DIM), lambda b, s, pt: (b, 0, 0)),
            pl.BlockSpec((rows, 128), lambda b, s, pt: (0, 0)),
            pl.BlockSpec((None, rows, SB_DIM), lambda b, s, pt: (b, 0, 0)),
            pl.BlockSpec((None, rows, SB_DIM), lambda b, s, pt: (b, 0, 0)),
            pl.BlockSpec((128, 256), lambda b, s, pt: (0, 0)),
        ] + [page_spec(p) for p in range(n_op)] + [page_spec(p) for p in range(n_op)],
        out_specs=pl.BlockSpec((None, rows, SB_DIM), lambda b, s, pt: (b, 0, 0)),
        scratch_shapes=[pltpu.VMEM((rows, 128), F32), pltpu.VMEM((rows, SB_DIM), F32)],
    )
    return pl.pallas_call(
        functools.partial(_sb_sample_kernel, n_op=n_op, t_new=t_new),
        grid_spec=grid_spec,
        out_shape=jax.ShapeDtypeStruct((n_b, rows, SB_DIM), F32),
        compiler_params=pltpu.CompilerParams(dimension_semantics=("arbitrary", "arbitrary"),
                                             vmem_limit_bytes=VMEM_LIMIT),
        name="sb_sample",
    )(page_table, q, bias_rows, kn, vn, mu, *([cache_k] * n_op), *([cache_v] * n_op))


def _sb_chains(chains, mu):
    logs = []
    for tiles, _, masks in chains:
        per_tile = []
        for z, mask in zip(tiles, masks):
            l = jnp.log2(1.0 + jnp.exp2(-jnp.abs(z)))
            log_beta = jnp.minimum(z, 0.0) - l
            log_1m = log_beta - z
            if mask is not None:
                log_1m = jnp.where(mask, log_1m, 0.0)
            per_tile.append((log_beta, log_1m.astype(BF16)))
        logs.append(per_tile)
    sums = [[[_dot(l16[:, s * 128:(s + 1) * 128], mu) for s in range(l16.shape[1] // 128)]
             for _, l16 in per_tile] for per_tile in logs]
    out = []
    for (tiles, c, masks), per_tile, per_tile_sums in zip(chains, logs, sums):
        atts = []
        for (log_beta, _), scs, mask in zip(per_tile, per_tile_sums, masks):
            xs = []
            for s in range(len(scs) - 1, -1, -1):
                xs.append(log_beta[:, s * 128:(s + 1) * 128] + scs[s][:, :128] + c)
                c = c + scs[s][:, 128:]
            att = jnp.exp2(xs[0] if len(xs) == 1 else jnp.concatenate(xs[::-1], axis=1))
            if mask is not None:
                att = jnp.where(mask, att, 0.0)
            atts.append(att.astype(BF16))
        out.append((atts, c))
    return out


def _sb_paged_kernel(pt_ref, qbd_ref, brow_ref, kn_ref, vn_ref, mu_ref, *rest, n_op, t_new):
    k_refs = rest[:n_op]
    v_refs = rest[n_op:2 * n_op]
    o_ref = rest[2 * n_op]
    c_scr, acc_scr = rest[2 * n_op + 1:]
    del pt_ref
    step = pl.program_id(1)
    qbd = qbd_ref[...]
    rows = qbd.shape[0]
    bias = brow_ref[...]
    mu = mu_ref[...]

    @pl.when(step == 0)
    def _():
        n_new = kn_ref.shape[0]
        pad = jnp.zeros((128 - n_new, SB_W), BF16)
        kn = jnp.concatenate([kn_ref[...].astype(BF16), pad], axis=0)
        vn = jnp.concatenate([vn_ref[...].astype(BF16), pad], axis=0)
        t_of_row = lax.broadcasted_iota(jnp.int32, (rows, 128), 0) // SB_HEADS
        key = lax.broadcasted_iota(jnp.int32, (rows, 128), 1)
        att, c = _sb_tile(_dot_nt(qbd, kn) + bias, jnp.zeros((rows, 128), F32), key < t_of_row, mu)
        c_scr[...] = c
        acc_scr[...] = _dot(att, vn)

    order = range(n_op - 1, -1, -1)
    zs = [_dot(qbd, k_refs[p][...].reshape(SB_W, PAGE_SIZE).astype(BF16)) + bias for p in order]
    (atts, c), = _sb_chains([(zs, c_scr[...], [None] * n_op)], mu)
    acc = acc_scr[...]
    for att, p in zip(atts, order):
        acc = acc + _dot_nt(att, v_refs[p][...].reshape(SB_W, PAGE_SIZE).astype(BF16))
    c_scr[...] = c
    acc_scr[...] = acc

    @pl.when(step == pl.num_programs(1) - 1)
    def _():
        h_of_row = lax.broadcasted_iota(jnp.int32, (rows, SB_W), 0) % SB_HEADS
        h_of_lane = lax.broadcasted_iota(jnp.int32, (rows, SB_W), 1) // SB_DIM
        diag = jnp.where(h_of_row == h_of_lane, acc, 0.0)
        o_ref[...] = jnp.sum(diag.reshape(t_new, SB_HEADS, SB_W), axis=1)


def _sb_paged(page_table, qbd, bias_rows, kn, vn, mu, cache_kt, cache_vt, layer, n_op):
    n_b, n_pages = page_table.shape
    rows = qbd.shape[1]
    t_new = rows // SB_HEADS
    n_steps = n_pages // n_op

    def page_spec(p):
        return pl.BlockSpec((None, None, SB_HEADS, SB_DIM, PAGE_SIZE),
                            lambda b, s, pt: (layer, pt[b, (n_steps - 1 - s) * n_op + p], 0, 0, 0))

    grid_spec = pltpu.PrefetchScalarGridSpec(
        num_scalar_prefetch=1,
        grid=(n_b, n_steps),
        in_specs=[
            pl.BlockSpec((None, rows, SB_W), lambda b, s, pt: (b, 0, 0)),
            pl.BlockSpec((rows, 128), lambda b, s, pt: (0, 0)),
            pl.BlockSpec((None, 8, SB_W), lambda b, s, pt: (b, 0, 0)),
            pl.BlockSpec((None, 8, SB_W), lambda b, s, pt: (b, 0, 0)),
            pl.BlockSpec((128, 256), lambda b, s, pt: (0, 0)),
        ] + [page_spec(p) for p in range(n_op)] + [page_spec(p) for p in range(n_op)],
        out_specs=pl.BlockSpec((None, t_new, SB_W), lambda b, s, pt: (b, 0, 0)),
        scratch_shapes=[pltpu.VMEM((rows, 128), F32), pltpu.VMEM((rows, SB_W), F32)],
    )
    return pl.pallas_call(
        functools.partial(_sb_paged_kernel, n_op=n_op, t_new=t_new),
        grid_spec=grid_spec,
        out_shape=jax.ShapeDtypeStruct((n_b, t_new, SB_W), F32),
        compiler_params=pltpu.CompilerParams(dimension_semantics=("arbitrary", "arbitrary"),
                                             vmem_limit_bytes=VMEM_LIMIT),
        name="sb_sample",
    )(page_table, qbd, bias_rows, kn, vn, mu, *([cache_kt] * n_op), *([cache_vt] * n_op))


def _regroup_w_in(w_in):
    sizes = (GLA_QK, GLA_QK, GLA_V, GLA_RANK, GLA_V, GDN_CONV_DIM, GDN_HEADS, GDN_HEADS, GDN_V, SB_W, SB_W, SB_W)
    offs = [0]
    for s in sizes:
        offs.append(offs[-1] + s)
    (g_q, g_k, g_v, g_lr, g_z, d_qkv, d_b, d_a, d_z, s_q, s_k, s_v) = [
        w_in[:, offs[i]:offs[i + 1]] for i in range(len(sizes))]
    d = w_in.shape[0]
    small = jnp.concatenate([g_lr, d_b, d_a, jnp.zeros((d, 128 - GLA_RANK - 2 * GDN_HEADS), w_in.dtype)], axis=1)
    return jnp.concatenate([s_q, s_k, s_v, d_qkv, g_v, g_z, d_z, g_q, g_k, small], axis=1).astype(BF16)


def _block_diag_expand(s, transpose):
    bsz, h, a, b = s.shape
    if transpose:
        s = jnp.swapaxes(s, 2, 3)
        a, b = b, a
    eye = jnp.eye(h, dtype=s.dtype)
    return (s[:, :, :, None, :] * eye[None, :, None, :, None]).reshape(bsz, h * a, h * b)


def _block_diag_extract(s, h, transpose):
    bsz, ra, rb = s.shape
    a, b = ra // h, rb // h
    s = s.reshape(bsz, h, a, h, b)
    s = jnp.stack([s[:, i, :, i, :] for i in range(h)], axis=1)
    return jnp.swapaxes(s, 2, 3) if transpose else s


def _group_matrix(n, group, scale):
    idx = jnp.arange(n) // group
    return (idx[:, None] == idx[None, :]).astype(F32) * scale


def _lane_expand(first_lane, heads, width):
    src = jnp.arange(128)[:, None]
    dst_h = jnp.arange(heads * width)[None, :] // width
    return (src == first_lane + dst_h).astype(F32)


def _suffix_matrix(tk):
    j = jnp.arange(tk)[:, None]
    s = jnp.arange(tk)[None, :]
    return jnp.concatenate([(j > s), jnp.ones((tk, tk), bool)], axis=1).astype(BF16)


def _pad_rows(a, n_seq, t, t_pad):
    w = a.shape[-1]
    a = a.reshape(n_seq, t, w)
    return jnp.pad(a, ((0, 0), (0, t_pad - t), (0, 0))).reshape(n_seq * t_pad, w)


def kernel(x_prompt, x_sample, cache_sb_k, cache_sb_v, page_table, state_gla, state_gdn, state_gdn_conv,
           ln1_g, w_in, gla_wa2, gla_ba, gla_norm_g, gdn_conv_w, gdn_a_log, gdn_dt_bias, gdn_norm_g,
           sb_q_norm_g, sb_k_norm_g, sb_logit_bias, w_out, ln2_g, w_up, w_down):
    depth = w_in.shape[0]
    bp, tp, d = x_prompt.shape
    bs, ts, _ = x_sample.shape
    ts_pad = 8
    assert ts <= ts_pad and ts >= CONV_K - 1 and tp % CHUNK == 0
    pages_per_step = min(SB_PAGES_PER_STEP, page_table.shape[1])
    assert page_table.shape[1] % pages_per_step == 0 and ts * SB_HEADS <= 128

    hp = x_prompt.reshape(bp * tp, d)
    hs = x_sample.reshape(bs * ts, d)
    cache_kt = jnp.transpose(cache_sb_k, (0, 1, 3, 4, 2))
    cache_vt = jnp.transpose(cache_sb_v, (0, 1, 3, 4, 2))

    gsum_b = _group_matrix(GDN_V, HEAD_DIM, 1.0).astype(BF16)
    mu = _suffix_matrix(PAGE_SIZE)
    sample_group = math.gcd(bs, SAMPLE_GROUP)
    zeros_gla_p = jnp.zeros((bp, GLA_V, GLA_QK), F32)
    zeros_gdn_p = jnp.zeros((bp, GDN_QK, GDN_V), F32)
    zeros_cb_p = jnp.zeros((bp, 8, GDN_CONV_DIM), F32)

    outs_p, outs_s = [], []
    for l in range(depth):
        w_pad = _regroup_w_in(w_in[l])
        wo = w_out[l].astype(BF16)
        wu = w_up[l].astype(BF16)
        wd = w_down[l].astype(BF16)
        ln1 = ln1_g[l][None, :]
        ln2 = ln2_g[l][None, :]
        qg = jnp.tile(sb_q_norm_g[l], SB_HEADS)[None, :]
        kg = jnp.tile(sb_k_norm_g[l], SB_HEADS)[None, :]
        wa2_pad = jnp.zeros((128, GLA_QK), F32).at[LANE_LR:LANE_LR + GLA_RANK].set(gla_wa2[l])
        ba = gla_ba[l][None, :]
        gla_ng = jnp.tile(gla_norm_g[l], GLA_HEADS)[None, :]
        gdn_ng = jnp.tile(gdn_norm_g[l], GDN_HEADS)[None, :]
        alog_row = jnp.zeros((1, 128), F32).at[0, LANE_DA:LANE_DA + GDN_HEADS].set(gdn_a_log[l])
        dtb_row = jnp.zeros((1, 128), F32).at[0, LANE_DA:LANE_DA + GDN_HEADS].set(gdn_dt_bias[l])
        cw = gdn_conv_w[l]
        bias = sb_logit_bias[l] * LOG2E

        qn, sk, sv, rest = _in_proj(hp, ln1, w_pad, qg, kg, 512)
        rest3 = rest.reshape(bp, tp, R_WIDTH)
        og, gla_st = _gla_group(rest3, zeros_gla_p, wa2_pad, ba, gla_ng, gsum_b, bp, CHUNK, CHUNK)
        od, gdn_st = _gdn_group(rest3, zeros_cb_p, zeros_gdn_p, cw, alog_row, dtb_row, gdn_ng, gsum_b,
                                bp, CHUNK, CHUNK)
        og = og.reshape(bp * tp, GLA_V)
        od = od.reshape(bp * tp, GDN_V)
        osb = _sb_prompt(qn, sk, sv, bias, mu, bp, tp, min(SB_TQ, tp), min(SB_TK, tp))
        hp = _out_ffn(hp, og, od, osb, wo, ln2, wu, wd, 512)
        conv_p = rest.reshape(bp, tp, R_WIDTH)[:, tp - (CONV_K - 1):, R_DQKV:R_DQKV + GDN_CONV_DIM]
        outs_p.append((sk.reshape(bp, tp, SB_HEADS, SB_DIM), sv.reshape(bp, tp, SB_HEADS, SB_DIM),
                       _block_diag_extract(gla_st, GLA_HEADS, True),
                       _block_diag_extract(gdn_st, GDN_HEADS, False), conv_p))

        qn, sk, sv, rest = _in_proj(hs, ln1, w_pad, qg, kg, bs * ts)
        rest_pad = jnp.pad(rest.reshape(bs, ts, R_WIDTH), ((0, 0), (0, ts_pad - ts), (0, 0)))
        og, gla_st = _gla_group(rest_pad, _block_diag_expand(state_gla[l], True), wa2_pad, ba, gla_ng, gsum_b,
                                sample_group, ts_pad, ts)
        cb0 = jnp.pad(state_gdn_conv[l], ((0, 0), (8 - (CONV_K - 1), 0), (0, 0)))
        od, gdn_st = _gdn_group(rest_pad, cb0, _block_diag_expand(state_gdn[l], False), cw, alog_row, dtb_row,
                                gdn_ng, gsum_b, sample_group, ts_pad, ts)
        og = og[:, :ts].reshape(bs * ts, GLA_V)
        od = od[:, :ts].reshape(bs * ts, GDN_V)
        bias_rows = jnp.broadcast_to(jnp.tile(bias, ts)[:, None], (ts * SB_HEADS, 128)).astype(F32)
        q4 = qn.reshape(bs, ts, 1, SB_HEADS, SB_DIM)
        eye = jnp.eye(SB_HEADS, dtype=BF16)[None, None, :, :, None]
        qbd = (q4 * eye).reshape(bs, ts * SB_HEADS, SB_W)
        kn = jnp.pad(sk.reshape(bs, ts, SB_W), ((0, 0), (0, 8 - ts), (0, 0)))
        vn = jnp.pad(sv.reshape(bs, ts, SB_W), ((0, 0), (0, 8 - ts), (0, 0)))
        osb = _sb_paged(page_table, qbd, bias_rows, kn, vn, mu, cache_kt, cache_vt, l, pages_per_step)
        osb = osb.reshape(bs * ts, SB_W).astype(BF16)
        hs = _out_ffn(hs, og, od, osb, wo, ln2, wu, wd, bs * ts)
        xp = jnp.concatenate([state_gdn_conv[l], rest.reshape(bs, ts, R_WIDTH)[:, :, R_DQKV:R_DQKV + GDN_CONV_DIM]],
                             axis=1)
        conv_s = xp[:, xp.shape[1] - (CONV_K - 1):]
        outs_s.append((sk.reshape(bs, ts, SB_HEADS, SB_DIM), sv.reshape(bs, ts, SB_HEADS, SB_DIM),
                       _block_diag_extract(gla_st, GLA_HEADS, True),
                       _block_diag_extract(gdn_st, GDN_HEADS, False), conv_s))

    sbk_p, sbv_p, gla_p, gdn_p, conv_p = [jnp.stack(a) for a in zip(*outs_p)]
    sbk_s, sbv_s, gla_s, gdn_s, conv_s = [jnp.stack(a) for a in zip(*outs_s)]
    return (hp.reshape(bp, tp, d), hs.reshape(bs, ts, d), sbk_p, sbv_p, gla_p, gdn_p, conv_p,
            sbk_s, sbv_s, gla_s, gdn_s, conv_s)
```

```python
import functools
import math

import jax
import jax.numpy as jnp
from jax import lax
from jax.experimental import pallas as pl
from jax.experimental.pallas import tpu as pltpu

F32 = jnp.float32
BF16 = jnp.bfloat16

LANES = 128
SUBLANES = 8

EPS = 1e-6
HEAD_DIM = 64
GLA_HEADS = 4
GLA_DK = 32
GLA_DV = 64
GLA_RANK = 16
GLA_TAU = 16.0
GDN_HEADS = 4
GDN_DK = 64
GDN_DV = 64
CONV_K = 4
SB_HEADS = 8
SB_DIM = 64
PAGE_SIZE = 128

TOKEN_ROWS = 512
CHUNK = 64
SB_TQ = 2048
SB_TK = 512
SB_PAGES_PER_STEP = 16
SAMPLE_GROUP = 16
SAMPLE_T_PAD = SUBLANES

GLA_QK = GLA_HEADS * GLA_DK
GLA_V = GLA_HEADS * GLA_DV
GDN_QK = GDN_HEADS * GDN_DK
GDN_V = GDN_HEADS * GDN_DV
GDN_CONV_DIM = 2 * GDN_QK + GDN_V
SB_W = SB_HEADS * SB_DIM

R_DQKV = 0
R_GV = 768
R_GZ = 1024
R_DZ = 1280
R_GQ = 1536
R_GK = 1664
R_SMALL = 1792
R_WIDTH = 1920
R_CHUNK = 640
LANE_LR = 0
LANE_DB = 16
LANE_DA = 20

VMEM_LIMIT = 56 * 1024 * 1024
LOG2E = 1.4426950408889634
SB_Q_SCALE = LOG2E * SB_DIM ** -0.5


def _dot(a, b):
    return jnp.dot(a, b, preferred_element_type=F32)


def _dot_nt(a, b):
    return lax.dot_general(a, b, (((1,), (1,)), ((), ())), preferred_element_type=F32)


def _dot_tn(a, b):
    return lax.dot_general(a, b, (((0,), (0,)), ((), ())), preferred_element_type=F32)


def _split(x):
    hi = x.astype(BF16)
    return hi, (x - hi.astype(F32)).astype(BF16)


def _dot_x3(a, b):
    ah, al = _split(a)
    bh, bl = _split(b)
    return _dot(ah, bh) + (_dot(al, bh) + _dot(ah, bl))


def _dot_x2(a, b_exact):
    ah, al = _split(a)
    return _dot(ah, b_exact) + _dot(al, b_exact)


def _dot_x2_lhs_exact(a_exact, b):
    bh, bl = _split(b)
    return _dot(a_exact, bh) + _dot(a_exact, bl)


def _softplus(x):
    return jnp.maximum(x, 0.0) + jnp.log1p(jnp.exp(-jnp.abs(x)))


def _sigmoid(x):
    return 1.0 / (1.0 + jnp.exp(-x))


def _silu(x):
    return x * _sigmoid(x)


def _row_col(c):
    row = lax.broadcasted_iota(jnp.int32, (c, c), 0)
    col = lax.broadcasted_iota(jnp.int32, (c, c), 1)
    return row, col


def _headnorm(x, g):
    outs = []
    lane = lax.broadcasted_iota(jnp.int32, (1, LANES), 1)
    first = lane < HEAD_DIM
    for j in range(x.shape[1] // LANES):
        xb = x[:, j * LANES:(j + 1) * LANES]
        sq = xb * xb
        s0 = jnp.sum(jnp.where(first, sq, 0.0), axis=-1, keepdims=True)
        s1 = jnp.sum(jnp.where(first, 0.0, sq), axis=-1, keepdims=True)
        ms = jnp.where(first, s0, s1) * (1.0 / HEAD_DIM)
        outs.append(xb * lax.rsqrt(ms + EPS))
    return jnp.concatenate(outs, axis=-1) * g


def _in_proj_kernel(x_ref, g_ref, w_ref, qg_ref, kg_ref, q_ref, k_ref, v_ref, r_ref):
    x = x_ref[...]
    ms = jnp.mean(x * x, axis=-1, keepdims=True)
    xn = (x * lax.rsqrt(ms + EPS) * g_ref[...]).astype(BF16)
    sq = _dot(xn, w_ref[:, 0:SB_W])
    q_ref[...] = (_headnorm(sq, qg_ref[...]) * SB_Q_SCALE).astype(BF16)
    sk = _dot(xn, w_ref[:, SB_W:2 * SB_W])
    k_ref[...] = _headnorm(sk, kg_ref[...])
    v_ref[...] = _dot(xn, w_ref[:, 2 * SB_W:3 * SB_W])
    for c0 in range(0, R_WIDTH, R_CHUNK):
        r_ref[:, c0:c0 + R_CHUNK] = _dot(xn, w_ref[:, 3 * SB_W + c0:3 * SB_W + c0 + R_CHUNK])


def _in_proj(h, ln_g, w_pad, qn_g, kn_g, tm):
    m = h.shape[0]
    row = lambda i: (i, 0)
    const = lambda i: (0, 0)
    return pl.pallas_call(
        _in_proj_kernel,
        grid=(m // tm,),
        in_specs=[
            pl.BlockSpec((tm, h.shape[1]), row),
            pl.BlockSpec((1, h.shape[1]), const),
            pl.BlockSpec(w_pad.shape, const, pipeline_mode=pl.Buffered(1)),
            pl.BlockSpec((1, SB_W), const),
            pl.BlockSpec((1, SB_W), const),
        ],
        out_specs=[
            pl.BlockSpec((tm, SB_W), row),
            pl.BlockSpec((tm, SB_W), row),
            pl.BlockSpec((tm, SB_W), row),
            pl.BlockSpec((tm, R_WIDTH), row),
        ],
        out_shape=[
            jax.ShapeDtypeStruct((m, SB_W), BF16),
            jax.ShapeDtypeStruct((m, SB_W), F32),
            jax.ShapeDtypeStruct((m, SB_W), F32),
            jax.ShapeDtypeStruct((m, R_WIDTH), F32),
        ],
        compiler_params=pltpu.CompilerParams(dimension_semantics=("arbitrary",), vmem_limit_bytes=VMEM_LIMIT),
        name="in_proj",
    )(h, ln_g, w_pad, qn_g, kn_g)


def _out_ffn_kernel(h_ref, og_ref, od_ref, os_ref, wo_ref, g_ref, wu_ref, wd_ref, o_ref, *, f_chunk):
    mixed = jnp.concatenate([og_ref[...], od_ref[...], os_ref[...]], axis=-1)
    h = h_ref[...] + _dot(mixed, wo_ref[...])
    ms = jnp.mean(h * h, axis=-1, keepdims=True)
    hn = (h * lax.rsqrt(ms + EPS) * g_ref[...]).astype(BF16)
    d_ff = wu_ref.shape[1]
    acc = h
    for f0 in range(0, d_ff, f_chunk):
        u = _dot(hn, wu_ref[:, f0:f0 + f_chunk])
        u = jnp.maximum(u, 0.0)
        acc = acc + _dot((u * u).astype(BF16), wd_ref[f0:f0 + f_chunk, :])
    o_ref[...] = acc


def _out_ffn(h, og, od, osb, wo, ln_g, wu, wd, tm):
    m, d = h.shape
    row = lambda i: (i, 0)
    const = lambda i: (0, 0)
    return pl.pallas_call(
        functools.partial(_out_ffn_kernel, f_chunk=1024),
        grid=(m // tm,),
        in_specs=[
            pl.BlockSpec((tm, d), row),
            pl.BlockSpec((tm, GLA_V), row),
            pl.BlockSpec((tm, GDN_V), row),
            pl.BlockSpec((tm, SB_W), row),
            pl.BlockSpec(wo.shape, const, pipeline_mode=pl.Buffered(1)),
            pl.BlockSpec((1, d), const),
            pl.BlockSpec(wu.shape, const, pipeline_mode=pl.Buffered(1)),
            pl.BlockSpec(wd.shape, const, pipeline_mode=pl.Buffered(1)),
        ],
        out_specs=pl.BlockSpec((tm, d), row),
        out_shape=jax.ShapeDtypeStruct((m, d), F32),
        compiler_params=pltpu.CompilerParams(dimension_semantics=("arbitrary",), vmem_limit_bytes=VMEM_LIMIT),
        name="out_ffn",
    )(h, og, od, osb, wo, ln_g, wu, wd)


def _gla_chunk(q, k, v, z, sm, st, wa2, ba, ng, gsum_b, *, c, t_valid):
    q = q * (GLA_DK ** -0.5)
    x = _dot_x3(sm, wa2) + ba
    log_a = -_softplus(-x) * (1.0 / GLA_TAU)
    if t_valid < c:
        valid = lax.broadcasted_iota(jnp.int32, (c, 1), 0) < t_valid
        log_a = jnp.where(valid, log_a, 0.0)
        q = jnp.where(valid, q, 0.0)
        k = jnp.where(valid, k, 0.0)
        v = jnp.where(valid, v, 0.0)
    row, col = _row_col(c)
    incl = row >= col
    b = _dot_x2_lhs_exact(incl.astype(BF16), log_a)
    b_last = b[c - 1:c, :]
    b_mid = b[c // 2 - 1:c // 2, :]
    q_in = (q * jnp.exp(b)).astype(BF16)
    q_t = q * jnp.exp(b - b_mid)
    k_t = (k * jnp.exp(b_mid - b)).astype(BF16)
    k_st = (k * jnp.exp(b_last - b)).astype(BF16)
    vb = v.astype(BF16)

    o = _dot_nt(q_in, st.astype(BF16))
    lane_k = lax.broadcasted_iota(jnp.int32, (1, GLA_QK), 1) // GLA_DK
    lane_v = lax.broadcasted_iota(jnp.int32, (1, GLA_V), 1) // GLA_DV
    for h in range(GLA_HEADS):
        qh = jnp.where(lane_k == h, q_t, 0.0).astype(BF16)
        att = jnp.where(incl, _dot_nt(qh, k_t), 0.0).astype(BF16)
        o = o + _dot(att, jnp.where(lane_v == h, vb, jnp.zeros_like(vb)))

    bd = (lax.broadcasted_iota(jnp.int32, (GLA_V, GLA_QK), 0) // GLA_DV
          == lax.broadcasted_iota(jnp.int32, (GLA_V, GLA_QK), 1) // GLA_DK)
    st_new = st * jnp.exp(b_last) + jnp.where(bd, _dot_tn(vb, k_st), 0.0)
    ms = _dot_x2(o * o, gsum_b) * (1.0 / GLA_DV)
    y = o * lax.rsqrt(ms + EPS) * ng
    return (y * _silu(z)).astype(BF16), st_new


def _gla_group_kernel(q_ref, k_ref, v_ref, z_ref, sm_ref, s0_ref, wa2_ref, ba_ref, ng_ref, gsum_ref,
                      o_ref, sout_ref, st_scr, *, group, chunk, t_valid):
    @pl.when(pl.program_id(1) == 0)
    def _():
        st_scr[...] = s0_ref[...]

    for g in range(group):
        y, st_new = _gla_chunk(q_ref[g], k_ref[g], v_ref[g], z_ref[g], sm_ref[g], st_scr[g], wa2_ref[...],
                               ba_ref[...], ng_ref[...], gsum_ref[...], c=chunk, t_valid=t_valid)
        o_ref[g] = y
        st_scr[g] = st_new
        sout_ref[g] = st_new


def _gla_group(rest3, s0t, wa2_pad, ba, ng_t, gsum_b, group, chunk, t_valid):
    n_seq, seq_len, _ = rest3.shape
    nc = seq_len // chunk
    blk = lambda width, cb: pl.BlockSpec((group, chunk, width), lambda b, c: (b, c, cb))
    const2 = lambda b, c: (0, 0)
    return pl.pallas_call(
        functools.partial(_gla_group_kernel, group=group, chunk=chunk, t_valid=t_valid),
        grid=(n_seq // group, nc),
        in_specs=[
            blk(GLA_QK, R_GQ // GLA_QK),
            blk(GLA_QK, R_GK // GLA_QK),
            blk(GLA_V, R_GV // GLA_V),
            blk(GLA_V, R_GZ // GLA_V),
            blk(LANES, R_SMALL // LANES),
            pl.BlockSpec((group, GLA_V, GLA_QK), lambda b, c: (b, 0, 0)),
            pl.BlockSpec((LANES, GLA_QK), const2),
            pl.BlockSpec((1, GLA_QK), const2),
            pl.BlockSpec((1, GLA_V), const2),
            pl.BlockSpec((GLA_V, GLA_V), const2),
        ],
        out_specs=[
            pl.BlockSpec((group, chunk, GLA_V), lambda b, c: (b, c, 0)),
            pl.BlockSpec((group, GLA_V, GLA_QK), lambda b, c: (b, 0, 0)),
        ],
        out_shape=[
            jax.ShapeDtypeStruct((n_seq, seq_len, GLA_V), BF16),
            jax.ShapeDtypeStruct((n_seq, GLA_V, GLA_QK), F32),
        ],
        scratch_shapes=[pltpu.VMEM((group, GLA_V, GLA_QK), F32)],
        compiler_params=pltpu.CompilerParams(dimension_semantics=("arbitrary", "arbitrary"),
                                             vmem_limit_bytes=VMEM_LIMIT),
        name="gla",
    )(rest3, rest3, rest3, rest3, rest3, s0t, wa2_pad, ba, ng_t, gsum_b)


def _per_head_lanes(cols, first_lane, heads, width):
    c = cols.shape[0]
    lane_h = lax.broadcasted_iota(jnp.int32, (1, heads * width), 1) // width
    out = jnp.broadcast_to(cols[:, first_lane:first_lane + 1], (c, heads * width))
    for h in range(1, heads):
        out = jnp.where(lane_h == h, jnp.broadcast_to(cols[:, first_lane + h:first_lane + h + 1], (c, heads * width)), out)
    return out


def _gdn_pre(xp_ref, x, sm, cw, alog, dtb, gsum_b, *, c, t_valid):
    xp_ref[SUBLANES:SUBLANES + c, :] = x
    base = SUBLANES - (CONV_K - 1)
    conv = xp_ref[base:base + c, :] * cw[0:1, :]
    for i in range(1, CONV_K):
        conv = conv + xp_ref[base + i:base + i + c, :] * cw[i:i + 1, :]
    xp_ref[0:SUBLANES, :] = xp_ref[c:c + SUBLANES, :]
    conv = _silu(conv)
    cq = conv[:, 0:GDN_QK]
    ck = conv[:, GDN_QK:2 * GDN_QK]
    v = conv[:, 2 * GDN_QK:]
    q = cq * lax.rsqrt(_dot_x2(cq * cq, gsum_b) + EPS) * (GDN_DK ** -0.5)
    k = ck * lax.rsqrt(_dot_x2(ck * ck, gsum_b) + EPS)

    beta_all = _sigmoid(sm)
    g_all = -jnp.exp(alog) * _softplus(sm + dtb)
    if t_valid < c:
        valid = lax.broadcasted_iota(jnp.int32, (c, 1), 0) < t_valid
        q = jnp.where(valid, q, 0.0)
        k = jnp.where(valid, k, 0.0)
        v = jnp.where(valid, v, 0.0)
        beta_all = jnp.where(valid, beta_all, 0.0)
        g_all = jnp.where(valid, g_all, 0.0)
    row, col = _row_col(c)
    b_col = _dot_x2_lhs_exact((row >= col).astype(BF16), g_all)
    bfull = _per_head_lanes(b_col, LANE_DA, GDN_HEADS, GDN_DV)
    beta_full = _per_head_lanes(beta_all, LANE_DB, GDN_HEADS, GDN_DV)
    b_last = bfull[c - 1:c, :]
    eb = jnp.exp(bfull)
    kb = k * beta_full
    return dict(
        b_col=b_col, b_row=b_col.T, b_last=b_last,
        q_in=(q * eb).astype(BF16), k_st=(k * jnp.exp(b_last - bfull)).astype(BF16),
        kbf=k.astype(BF16), kbb=kb.astype(BF16), qb=q.astype(BF16),
        rhs=jnp.concatenate([v * beta_full, kb * eb], axis=1).astype(BF16))


def _gdn_post(pre, uw, atts, s, z, ng, gsum_b):
    lane_h = lax.broadcasted_iota(jnp.int32, (1, GDN_V), 1) // GDN_DV
    sb = s.astype(BF16)
    v_new = uw[:, :GDN_V] - _dot(uw[:, GDN_V:].astype(BF16), sb)
    vnb = v_new.astype(BF16)
    o = _dot(pre["q_in"], sb)
    for h in range(GDN_HEADS):
        o = o + _dot(atts[h], jnp.where(lane_h == h, vnb, jnp.zeros_like(vnb)))
    bd = (lax.broadcasted_iota(jnp.int32, (GDN_QK, GDN_V), 0) // GDN_DK
          == lax.broadcasted_iota(jnp.int32, (GDN_QK, GDN_V), 1) // GDN_DV)
    s_new = s * jnp.exp(pre["b_last"]) + jnp.where(bd, _dot_tn(pre["k_st"], vnb), 0.0)
    ms = _dot_x2(o * o, gsum_b) * (1.0 / GDN_DV)
    y = o * lax.rsqrt(ms + EPS) * ng
    return (y * _silu(z)).astype(BF16), s_new


def _gdn_group_kernel(x_ref, sm_ref, z_ref, cb0_ref, s0_ref, cw_ref, alog_ref, dtb_ref, ng_ref, gsum_ref,
                      o_ref, sout_ref, xp_scr, s_scr, *, group, chunk, t_valid):
    c = chunk

    @pl.when(pl.program_id(1) == 0)
    def _():
        xp_scr[:, 0:SUBLANES, :] = cb0_ref[...]
        s_scr[...] = s0_ref[...]

    gsum_b = gsum_ref[...]
    pre = [_gdn_pre(xp_scr.at[g], x_ref[g], sm_ref[g], cw_ref[...], alog_ref[...], dtb_ref[...], gsum_b,
                    c=c, t_valid=t_valid) for g in range(group)]

    row, col = _row_col(c)
    incl = row >= col
    strict = row > col
    eye = (row == col).astype(F32)
    lane_h = lax.broadcasted_iota(jnp.int32, (1, GDN_V), 1) // GDN_DV
    lane_h2 = lax.broadcasted_iota(jnp.int32, (1, 2 * GDN_V), 1) % GDN_V // GDN_DV
    chains = [(g, h) for g in range(group) for h in range(GDN_HEADS)]
    dec, qq, p, atts = {}, {}, {}, {}
    for g, h in chains:
        pg = pre[g]
        b_t = jnp.broadcast_to(pg["b_col"][:, LANE_DA + h:LANE_DA + h + 1], (c, c))
        b_s = jnp.broadcast_to(pg["b_row"][LANE_DA + h:LANE_DA + h + 1, :], (c, c))
        dec[g, h] = jnp.exp(jnp.minimum(b_t - b_s, 0.0))
    for g, h in chains:
        pg = pre[g]
        kbh = jnp.where(lane_h == h, pg["kbb"], jnp.zeros_like(pg["kbb"]))
        a = jnp.where(strict, _dot_nt(kbh, pg["kbf"]) * dec[g, h], 0.0)
        qq[g, h] = a
        p[g, h] = eye - a
    for _ in range(int(round(math.log2(c))) - 1):
        for ch in chains:
            qq[ch] = _dot_x3(qq[ch], qq[ch])
        for ch in chains:
            p[ch] = p[ch] + _dot_x3(p[ch], qq[ch])
    for g, h in chains:
        pg = pre[g]
        qh = jnp.where(lane_h == h, pg["qb"], jnp.zeros_like(pg["qb"]))
        atts[g, h] = jnp.where(incl, _dot_nt(qh, pg["kbf"]) * dec[g, h], 0.0).astype(BF16)
    uws = []
    for g in range(group):
        rhs = pre[g]["rhs"]
        uw = _dot(p[g, 0].astype(BF16), jnp.where(lane_h2 == 0, rhs, jnp.zeros_like(rhs)))
        for h in range(1, GDN_HEADS):
            uw = uw + _dot(p[g, h].astype(BF16), jnp.where(lane_h2 == h, rhs, jnp.zeros_like(rhs)))
        uws.append(uw)

    for g in range(group):
        y, s_new = _gdn_post(pre[g], uws[g], [atts[g, h] for h in range(GDN_HEADS)], s_scr[g], z_ref[g],
                             ng_ref[...], gsum_b)
        o_ref[g] = y
        s_scr[g] = s_new
        sout_ref[g] = s_new


def _gdn_group(rest3, cb0, s0, cw, alog_row, dtb_row, ng_t, gsum_b, group, chunk, t_valid):
    n_seq, seq_len, _ = rest3.shape
    nc = seq_len // chunk
    blk = lambda width, cb: pl.BlockSpec((group, chunk, width), lambda b, c: (b, c, cb))
    const2 = lambda b, c: (0, 0)
    return pl.pallas_call(
        functools.partial(_gdn_group_kernel, group=group, chunk=chunk, t_valid=t_valid),
        grid=(n_seq // group, nc),
        in_specs=[
            blk(GDN_CONV_DIM, R_DQKV // GDN_CONV_DIM),
            blk(LANES, R_SMALL // LANES),
            blk(GDN_V, R_DZ // GDN_V),
            pl.BlockSpec((group, SUBLANES, GDN_CONV_DIM), lambda b, c: (b, 0, 0)),
            pl.BlockSpec((group, GDN_QK, GDN_V), lambda b, c: (b, 0, 0)),
            pl.BlockSpec((CONV_K, GDN_CONV_DIM), const2),
            pl.BlockSpec((1, LANES), const2),
            pl.BlockSpec((1, LANES), const2),
            pl.BlockSpec((1, GDN_V), const2),
            pl.BlockSpec((GDN_V, GDN_V), const2),
        ],
        out_specs=[
            pl.BlockSpec((group, chunk, GDN_V), lambda b, c: (b, c, 0)),
            pl.BlockSpec((group, GDN_QK, GDN_V), lambda b, c: (b, 0, 0)),
        ],
        out_shape=[
            jax.ShapeDtypeStruct((n_seq, seq_len, GDN_V), BF16),
            jax.ShapeDtypeStruct((n_seq, GDN_QK, GDN_V), F32),
        ],
        scratch_shapes=[pltpu.VMEM((group, chunk + SUBLANES, GDN_CONV_DIM), F32),
                        pltpu.VMEM((group, GDN_QK, GDN_V), F32)],
        compiler_params=pltpu.CompilerParams(dimension_semantics=("arbitrary", "arbitrary"),
                                             vmem_limit_bytes=VMEM_LIMIT),
        name="gdn",
    )(rest3, rest3, rest3, cb0, s0, cw, alog_row, dtb_row, ng_t, gsum_b)


def _sb_chains(chains, mu):
    logs = []
    for tiles, _, masks in chains:
        per_tile = []
        for z, mask in zip(tiles, masks):
            l = jnp.log2(1.0 + jnp.exp2(-jnp.abs(z)))
            log_beta = jnp.minimum(z, 0.0) - l
            log_1m = log_beta - z
            if mask is not None:
                log_1m = jnp.where(mask, log_1m, 0.0)
            per_tile.append((log_beta, log_1m.astype(BF16)))
        logs.append(per_tile)
    sums = [[[_dot(l16[:, s * LANES:(s + 1) * LANES], mu) for s in range(l16.shape[1] // LANES)]
             for _, l16 in per_tile] for per_tile in logs]
    out = []
    for (tiles, c, masks), per_tile, per_tile_sums in zip(chains, logs, sums):
        atts = []
        for (log_beta, _), scs, mask in zip(per_tile, per_tile_sums, masks):
            xs = []
            for s in range(len(scs) - 1, -1, -1):
                xs.append(log_beta[:, s * LANES:(s + 1) * LANES] + scs[s][:, :LANES] + c)
                c = c + scs[s][:, LANES:]
            att = jnp.exp2(xs[0] if len(xs) == 1 else jnp.concatenate(xs[::-1], axis=1))
            if mask is not None:
                att = jnp.where(mask, att, 0.0)
            atts.append(att.astype(BF16))
        out.append((atts, c))
    return out


def _sb_prompt_kernel(bias_ref, q_ref, k_ref, v_ref, mu_ref, o_ref, kb_scr, vb_scr, c_scr, o_scr, *, tq, tk):
    hp = pl.program_id(1)
    i = pl.program_id(2)

    @pl.when(i == 0)
    def _():
        kb_scr[...] = k_ref[...].astype(BF16)
        vb_scr[...] = v_ref[...].astype(BF16)

    q = q_ref[...]
    lane = lax.broadcasted_iota(jnp.int32, (1, 2 * SB_DIM), 1)
    first = lane < SB_DIM
    zero = jnp.zeros_like(q)
    qs = (jnp.where(first, q, zero), jnp.where(first, zero, q))
    bias = (bias_ref[2 * hp], bias_ref[2 * hp + 1])
    mu = mu_ref[...]
    c_scr[...] = jnp.zeros_like(c_scr)
    o_scr[...] = jnp.zeros_like(o_scr)
    n_sub = tq // tk

    def tile(j, masked, r0):
        start = pl.multiple_of(j * tk, tk)
        kblk = kb_scr[pl.ds(start, tk), :]
        vblk = vb_scr[pl.ds(start, tk), :]
        mask = None
        if masked:
            q_pos = i * tq + r0 + lax.broadcasted_iota(jnp.int32, (tq - r0, tk), 0)
            mask = start + lax.broadcasted_iota(jnp.int32, (tq - r0, tk), 1) < q_pos
        zs = [_dot_nt(qs[hh][r0:], kblk) + bias[hh] for hh in range(2)]
        res = _sb_chains([([zs[hh]], c_scr[hh, r0:], [mask]) for hh in range(2)], mu)
        for hh in range(2):
            (att,), c = res[hh]
            c_scr[hh, r0:] = c
            o_scr[hh, r0:] = o_scr[hh, r0:] + _dot(att, vblk)

    for dd in range(n_sub - 1, -1, -1):
        tile(i * n_sub + dd, True, dd * tk)

    def body(jj, carry):
        tile(i * n_sub - 1 - jj, False, 0)
        return carry

    lax.fori_loop(0, i * n_sub, body, 0)
    o_ref[...] = jnp.where(first, o_scr[0], o_scr[1]).astype(BF16)


def _sb_prompt(qn, sk, sv, bias, mu, n_seq, seq_len, tq, tk):
    nq = seq_len // tq
    n_hp = SB_HEADS // 2
    grid_spec = pltpu.PrefetchScalarGridSpec(
        num_scalar_prefetch=1,
        grid=(n_seq, n_hp, nq),
        in_specs=[
            pl.BlockSpec((tq, 2 * SB_DIM), lambda b, hp, i, bias: (b * nq + i, hp)),
            pl.BlockSpec((seq_len, 2 * SB_DIM), lambda b, hp, i, bias: (b, hp)),
            pl.BlockSpec((seq_len, 2 * SB_DIM), lambda b, hp, i, bias: (b, hp)),
            pl.BlockSpec((LANES, 2 * LANES), lambda b, hp, i, bias: (0, 0)),
        ],
        out_specs=pl.BlockSpec((tq, 2 * SB_DIM), lambda b, hp, i, bias: (b * nq + i, hp)),
        scratch_shapes=[pltpu.VMEM((seq_len, 2 * SB_DIM), BF16), pltpu.VMEM((seq_len, 2 * SB_DIM), BF16),
                        pltpu.VMEM((2, tq, LANES), F32), pltpu.VMEM((2, tq, 2 * SB_DIM), F32)],
    )
    return pl.pallas_call(
        functools.partial(_sb_prompt_kernel, tq=tq, tk=tk),
        grid_spec=grid_spec,
        out_shape=jax.ShapeDtypeStruct((n_seq * seq_len, SB_W), BF16),
        compiler_params=pltpu.CompilerParams(dimension_semantics=("arbitrary", "arbitrary", "arbitrary"),
                                             vmem_limit_bytes=VMEM_LIMIT),
        name="sb_prompt",
    )(bias, qn, sk, sv, mu)


def _sb_paged_kernel(pt_ref, qbd_ref, brow_ref, kn_ref, vn_ref, mu_ref, *rest, n_op, t_new):
    k_refs = rest[:n_op]
    v_refs = rest[n_op:2 * n_op]
    o_ref = rest[2 * n_op]
    c_scr, acc_scr = rest[2 * n_op + 1:]
    del pt_ref
    step = pl.program_id(1)
    qbd = qbd_ref[...]
    rows = qbd.shape[0]
    bias = brow_ref[...]
    mu = mu_ref[...]

    @pl.when(step == 0)
    def _():
        n_new = kn_ref.shape[0]
        pad = jnp.zeros((PAGE_SIZE - n_new, SB_W), BF16)
        kn = jnp.concatenate([kn_ref[...].astype(BF16), pad], axis=0)
        vn = jnp.concatenate([vn_ref[...].astype(BF16), pad], axis=0)
        t_of_row = lax.broadcasted_iota(jnp.int32, (rows, PAGE_SIZE), 0) // SB_HEADS
        key = lax.broadcasted_iota(jnp.int32, (rows, PAGE_SIZE), 1)
        ((att,), c), = _sb_chains([([_dot_nt(qbd, kn) + bias], jnp.zeros((rows, LANES), F32), [key < t_of_row])], mu)
        c_scr[...] = c
        acc_scr[...] = _dot(att, vn)

    order = range(n_op - 1, -1, -1)
    zs = [_dot(qbd, k_refs[p][...].reshape(SB_W, PAGE_SIZE).astype(BF16)) + bias for p in order]
    (atts, c), = _sb_chains([(zs, c_scr[...], [None] * n_op)], mu)
    acc = acc_scr[...]
    for att, p in zip(atts, order):
        acc = acc + _dot_nt(att, v_refs[p][...].reshape(SB_W, PAGE_SIZE).astype(BF16))
    c_scr[...] = c
    acc_scr[...] = acc

    @pl.when(step == pl.num_programs(1) - 1)
    def _():
        h_of_row = lax.broadcasted_iota(jnp.int32, (rows, SB_W), 0) % SB_HEADS
        h_of_lane = lax.broadcasted_iota(jnp.int32, (rows, SB_W), 1) // SB_DIM
        diag = jnp.where(h_of_row == h_of_lane, acc, 0.0)
        o_ref[...] = jnp.sum(diag.reshape(t_new, SB_HEADS, SB_W), axis=1)


def _sb_paged(page_table, qbd, bias_rows, kn, vn, mu, cache_kt, cache_vt, layer, n_op):
    n_b, n_pages = page_table.shape
    rows = qbd.shape[1]
    t_new = rows // SB_HEADS
    n_steps = n_pages // n_op

    def page_spec(p):
        return pl.BlockSpec((None, None, SB_HEADS, SB_DIM, PAGE_SIZE),
                            lambda b, s, pt: (layer, pt[b, (n_steps - 1 - s) * n_op + p], 0, 0, 0))

    grid_spec = pltpu.PrefetchScalarGridSpec(
        num_scalar_prefetch=1,
        grid=(n_b, n_steps),
        in_specs=[
            pl.BlockSpec((None, rows, SB_W), lambda b, s, pt: (b, 0, 0)),
            pl.BlockSpec((rows, LANES), lambda b, s, pt: (0, 0)),
            pl.BlockSpec((None, SUBLANES, SB_W), lambda b, s, pt: (b, 0, 0)),
            pl.BlockSpec((None, SUBLANES, SB_W), lambda b, s, pt: (b, 0, 0)),
            pl.BlockSpec((LANES, 2 * LANES), lambda b, s, pt: (0, 0)),
        ] + [page_spec(p) for p in range(n_op)] + [page_spec(p) for p in range(n_op)],
        out_specs=pl.BlockSpec((None, t_new, SB_W), lambda b, s, pt: (b, 0, 0)),
        scratch_shapes=[pltpu.VMEM((rows, LANES), F32), pltpu.VMEM((rows, SB_W), F32)],
    )
    return pl.pallas_call(
        functools.partial(_sb_paged_kernel, n_op=n_op, t_new=t_new),
        grid_spec=grid_spec,
        out_shape=jax.ShapeDtypeStruct((n_b, t_new, SB_W), F32),
        compiler_params=pltpu.CompilerParams(dimension_semantics=("arbitrary", "arbitrary"),
                                             vmem_limit_bytes=VMEM_LIMIT),
        name="sb_sample",
    )(page_table, qbd, bias_rows, kn, vn, mu, *([cache_kt] * n_op), *([cache_vt] * n_op))


def _regroup_w_in(w_in):
    sizes = (GLA_QK, GLA_QK, GLA_V, GLA_RANK, GLA_V, GDN_CONV_DIM, GDN_HEADS, GDN_HEADS, GDN_V, SB_W, SB_W, SB_W)
    offs = [0]
    for s in sizes:
        offs.append(offs[-1] + s)
    (g_q, g_k, g_v, g_lr, g_z, d_qkv, d_b, d_a, d_z, s_q, s_k, s_v) = [
        w_in[:, offs[i]:offs[i + 1]] for i in range(len(sizes))]
    d = w_in.shape[0]
    small = jnp.concatenate([g_lr, d_b, d_a, jnp.zeros((d, LANES - GLA_RANK - 2 * GDN_HEADS), w_in.dtype)], axis=1)
    return jnp.concatenate([s_q, s_k, s_v, d_qkv, g_v, g_z, d_z, g_q, g_k, small], axis=1).astype(BF16)


def _block_diag_expand(s, transpose):
    bsz, h, a, b = s.shape
    if transpose:
        s = jnp.swapaxes(s, 2, 3)
        a, b = b, a
    eye = jnp.eye(h, dtype=s.dtype)
    return (s[:, :, :, None, :] * eye[None, :, None, :, None]).reshape(bsz, h * a, h * b)


def _block_diag_extract(s, h, transpose):
    bsz, ra, rb = s.shape
    a, b = ra // h, rb // h
    s = s.reshape(bsz, h, a, h, b)
    s = jnp.stack([s[:, i, :, i, :] for i in range(h)], axis=1)
    return jnp.swapaxes(s, 2, 3) if transpose else s


def _same_head_matrix(n, width):
    idx = jnp.arange(n) // width
    return (idx[:, None] == idx[None, :]).astype(BF16)


def _suffix_matrix(tk):
    j = jnp.arange(tk)[:, None]
    s = jnp.arange(tk)[None, :]
    return jnp.concatenate([(j > s), jnp.ones((tk, tk), bool)], axis=1).astype(BF16)


def kernel(x_prompt, x_sample, cache_sb_k, cache_sb_v, page_table, state_gla, state_gdn, state_gdn_conv,
           ln1_g, w_in, gla_wa2, gla_ba, gla_norm_g, gdn_conv_w, gdn_a_log, gdn_dt_bias, gdn_norm_g,
           sb_q_norm_g, sb_k_norm_g, sb_logit_bias, w_out, ln2_g, w_up, w_down):
    depth = w_in.shape[0]
    bp, tp, d = x_prompt.shape
    bs, ts, _ = x_sample.shape
    ts_pad = SAMPLE_T_PAD
    assert CONV_K - 1 <= ts <= ts_pad and tp % CHUNK == 0
    pages_per_step = min(SB_PAGES_PER_STEP, page_table.shape[1])
    assert page_table.shape[1] % pages_per_step == 0 and ts <= SUBLANES
    tm_p = min(TOKEN_ROWS, bp * tp)
    assert (bp * tp) % tm_p == 0

    hp = x_prompt.reshape(bp * tp, d)
    hs = x_sample.reshape(bs * ts, d)
    cache_kt = jnp.transpose(cache_sb_k, (0, 1, 3, 4, 2))
    cache_vt = jnp.transpose(cache_sb_v, (0, 1, 3, 4, 2))

    gsum_b = _same_head_matrix(GDN_V, HEAD_DIM)
    mu = _suffix_matrix(LANES)
    sample_group = math.gcd(bs, SAMPLE_GROUP)
    zeros_gla_p = jnp.zeros((bp, GLA_V, GLA_QK), F32)
    zeros_gdn_p = jnp.zeros((bp, GDN_QK, GDN_V), F32)
    zeros_cb_p = jnp.zeros((bp, SUBLANES, GDN_CONV_DIM), F32)

    outs_p, outs_s = [], []
    for l in range(depth):
        w_pad = _regroup_w_in(w_in[l])
        wo = w_out[l].astype(BF16)
        wu = w_up[l].astype(BF16)
        wd = w_down[l].astype(BF16)
        ln1 = ln1_g[l][None, :]
        ln2 = ln2_g[l][None, :]
        qg = jnp.tile(sb_q_norm_g[l], SB_HEADS)[None, :]
        kg = jnp.tile(sb_k_norm_g[l], SB_HEADS)[None, :]
        wa2_pad = jnp.zeros((LANES, GLA_QK), F32).at[LANE_LR:LANE_LR + GLA_RANK].set(gla_wa2[l])
        ba = gla_ba[l][None, :]
        gla_ng = jnp.tile(gla_norm_g[l], GLA_HEADS)[None, :]
        gdn_ng = jnp.tile(gdn_norm_g[l], GDN_HEADS)[None, :]
        alog_row = jnp.zeros((1, LANES), F32).at[0, LANE_DA:LANE_DA + GDN_HEADS].set(gdn_a_log[l])
        dtb_row = jnp.zeros((1, LANES), F32).at[0, LANE_DA:LANE_DA + GDN_HEADS].set(gdn_dt_bias[l])
        cw = gdn_conv_w[l]
        bias = sb_logit_bias[l] * LOG2E

        qn, sk, sv, rest = _in_proj(hp, ln1, w_pad, qg, kg, tm_p)
        rest3 = rest.reshape(bp, tp, R_WIDTH)
        og, gla_st = _gla_group(rest3, zeros_gla_p, wa2_pad, ba, gla_ng, gsum_b, bp, CHUNK, CHUNK)
        od, gdn_st = _gdn_group(rest3, zeros_cb_p, zeros_gdn_p, cw, alog_row, dtb_row, gdn_ng, gsum_b,
                                bp, CHUNK, CHUNK)
        og = og.reshape(bp * tp, GLA_V)
        od = od.reshape(bp * tp, GDN_V)
        osb = _sb_prompt(qn, sk, sv, bias, mu, bp, tp, min(SB_TQ, tp), min(SB_TK, tp))
        hp = _out_ffn(hp, og, od, osb, wo, ln2, wu, wd, tm_p)
        conv_p = rest3[:, tp - (CONV_K - 1):, R_DQKV:R_DQKV + GDN_CONV_DIM]
        outs_p.append((sk.reshape(bp, tp, SB_HEADS, SB_DIM), sv.reshape(bp, tp, SB_HEADS, SB_DIM),
                       _block_diag_extract(gla_st, GLA_HEADS, True),
                       _block_diag_extract(gdn_st, GDN_HEADS, False), conv_p))

        qn, sk, sv, rest = _in_proj(hs, ln1, w_pad, qg, kg, bs * ts)
        rest3 = rest.reshape(bs, ts, R_WIDTH)
        rest_pad = jnp.pad(rest3, ((0, 0), (0, ts_pad - ts), (0, 0)))
        og, gla_st = _gla_group(rest_pad, _block_diag_expand(state_gla[l], True), wa2_pad, ba, gla_ng, gsum_b,
                                sample_group, ts_pad, ts)
        cb0 = jnp.pad(state_gdn_conv[l], ((0, 0), (SUBLANES - (CONV_K - 1), 0), (0, 0)))
        od, gdn_st = _gdn_group(rest_pad, cb0, _block_diag_expand(state_gdn[l], False), cw, alog_row, dtb_row,
                                gdn_ng, gsum_b, sample_group, ts_pad, ts)
        og = og[:, :ts].reshape(bs * ts, GLA_V)
        od = od[:, :ts].reshape(bs * ts, GDN_V)
        bias_rows = jnp.broadcast_to(jnp.tile(bias, ts)[:, None], (ts * SB_HEADS, LANES)).astype(F32)
        q4 = qn.reshape(bs, ts, 1, SB_HEADS, SB_DIM)
        eye = jnp.eye(SB_HEADS, dtype=BF16)[None, None, :, :, None]
        qbd = (q4 * eye).reshape(bs, ts * SB_HEADS, SB_W)
        kn = jnp.pad(sk.reshape(bs, ts, SB_W), ((0, 0), (0, SUBLANES - ts), (0, 0)))
        vn = jnp.pad(sv.reshape(bs, ts, SB_W), ((0, 0), (0, SUBLANES - ts), (0, 0)))
        osb = _sb_paged(page_table, qbd, bias_rows, kn, vn, mu, cache_kt, cache_vt, l, pages_per_step)
        osb = osb.reshape(bs * ts, SB_W).astype(BF16)
        hs = _out_ffn(hs, og, od, osb, wo, ln2, wu, wd, bs * ts)
        xp = jnp.concatenate([state_gdn_conv[l], rest3[:, :, R_DQKV:R_DQKV + GDN_CONV_DIM]], axis=1)
        conv_s = xp[:, xp.shape[1] - (CONV_K - 1):]
        outs_s.append((sk.reshape(bs, ts, SB_HEADS, SB_DIM), sv.reshape(bs, ts, SB_HEADS, SB_DIM),
                       _block_diag_extract(gla_st, GLA_HEADS, True),
                       _block_diag_extract(gdn_st, GDN_HEADS, False), conv_s))

    sbk_p, sbv_p, gla_p, gdn_p, conv_p = [jnp.stack(a) for a in zip(*outs_p)]
    sbk_s, sbv_s, gla_s, gdn_s, conv_s = [jnp.stack(a) for a in zip(*outs_s)]
    return (hp.reshape(bp, tp, d), hs.reshape(bs, ts, d), sbk_p, sbv_p, gla_p, gdn_p, conv_p,
            sbk_s, sbv_s, gla_s, gdn_s, conv_s)
```
